```python
import jax, jax.numpy as jnp
from jax import lax
import numpy as np

D_MODEL = 2048
BATCH = 2
SEQ = 16384
DEPTH = 2
DEC_BATCH = 16
DEC_SEQ = 16
PAST_LEN = 1024

CHUNK = 64
QBLK = 128
N_BRANCH = 4
HA = 4
DHA = 128
WA = HA * DHA
WB = 512
GB = 4
GC = WB // GB
GM_CHUNK = 128
HC = 4
NOPE = 128
ROPE_DIM = 64
VDIM = 128
Q_RANK = 384
KV_RANK = 256
WC = HC * VDIM
WD = 512
CONV_W = 3
ROPE_BASE = 10000.0
FORGET_BIAS = 3.0
EPS = 1e-6
NEG_INF = -1e30
SEG_SIZES = (WA, WA, WA, HA, WA, WB, WB, WB, Q_RANK, KV_RANK, ROPE_DIM, WC, WD, WD, WD, WD, N_BRANCH * D_MODEL)
N_IN = sum(SEG_SIZES)

kernel_name = "hybrid_fox_gmlp_mla_shortconv_stream_step"


def _split_points():
    pts, acc = [], 0
    for s in SEG_SIZES[:-1]:
        acc += s
        pts.append(acc)
    return pts


def _rmsnorm(x, g):
    xf = x.astype(jnp.float32)
    y = xf * lax.rsqrt(jnp.mean(xf * xf, axis=-1, keepdims=True) + EPS)
    return (y * g.astype(jnp.float32)).astype(x.dtype)


def _layernorm(x, g, b):
    xf = x.astype(jnp.float32)
    mu = jnp.mean(xf, axis=-1, keepdims=True)
    xc = xf - mu
    var = jnp.mean(xc * xc, axis=-1, keepdims=True)
    return (xc * lax.rsqrt(var + EPS) * g.astype(jnp.float32) + b.astype(jnp.float32)).astype(x.dtype)


def _rope(x, pos):
    half = x.shape[-1] // 2
    inv = ROPE_BASE ** (-jnp.arange(half, dtype=jnp.float32) / half)
    ang = pos.astype(jnp.float32)[:, None] * inv[None, :]
    shape = (1, pos.shape[0]) + (1,) * (x.ndim - 3) + (half,)
    cos, sin = jnp.cos(ang).reshape(shape), jnp.sin(ang).reshape(shape)
    x1 = x[..., :half].astype(jnp.float32)
    x2 = x[..., half:].astype(jnp.float32)
    return jnp.concatenate([x1 * cos - x2 * sin, x2 * cos + x1 * sin], axis=-1).astype(x.dtype)


def _attend(s, mask, v):
    p = jax.nn.softmax(jnp.where(mask, s, NEG_INF), axis=-1)
    return jnp.einsum("bhqk,bkhd->bqhd", p.astype(v.dtype), v)


def _fox_core(q, k, v, cq, ckT, qpos, kpos):
    s = jnp.einsum("bqhd,bkhd->bhqk", q, k, preferred_element_type=jnp.float32) * (DHA ** -0.5)
    s = s + (jnp.transpose(cq, (0, 2, 1))[..., :, None] - ckT[..., None, :])
    mask = kpos[None, :] <= qpos[:, None]
    return _attend(s, mask, v)


def _mla_core(q_nope, q_rope, k_nope, v, k_rope, qpos, kpos):
    s = (jnp.einsum("bqhd,bkhd->bhqk", q_nope, k_nope, preferred_element_type=jnp.float32)
         + jnp.einsum("bqhr,bkr->bhqk", q_rope, k_rope, preferred_element_type=jnp.float32))
    s = s * ((NOPE + ROPE_DIM) ** -0.5)
    mask = (kpos // CHUNK)[None, :] <= (qpos // CHUNK)[:, None]
    return _attend(s, mask, v)


def _query_blocks(fn, qpos, *qs):
    S = qpos.shape[0]
    nb = S // QBLK
    blk = lambda a: jnp.moveaxis(a.reshape((a.shape[0], nb, QBLK) + a.shape[2:]), 1, 0)
    out = lax.map(lambda args: fn(*args), (qpos.reshape(nb, QBLK),) + tuple(blk(a) for a in qs))
    out = jnp.moveaxis(out, 0, 1)
    return out.reshape((out.shape[0], S) + out.shape[3:])


def _layer(x, pos, lp, cache):
    B, T, _ = x.shape
    keep = min(T, PAST_LEN)
    h = _rmsnorm(x, lp["norm_g"])
    z = jnp.einsum("btd,dn->btn", h, lp["w_in"])
    (a_q, a_k, a_v, a_f, a_g, b_u, b_v, b_g, c_q, c_kv, c_kr, c_g,
     d_b, d_c, d_h, d_g, m_g) = jnp.split(z, _split_points(), axis=-1)

    q = a_q.reshape(B, T, HA, DHA)
    k = a_k.reshape(B, T, HA, DHA)
    v = a_v.reshape(B, T, HA, DHA)
    logf = jax.nn.log_sigmoid(a_f.astype(jnp.float32) + lp["fox_fb"].astype(jnp.float32))
    if cache is None:
        kpos = jnp.arange(T, dtype=jnp.int32)
        c = jnp.cumsum(logf, axis=1)
        ckT = jnp.transpose(c, (0, 2, 1))
        o_a = _query_blocks(lambda qp, qb, cb: _fox_core(qb, k, v, cb, ckT, qp, kpos), pos, q, c)
        st_fox = (k[:, T - keep:], v[:, T - keep:], logf[:, T - keep:])
    else:
        P = cache["fox_k"].shape[1]
        k_all = jnp.concatenate([cache["fox_k"], k], axis=1)
        v_all = jnp.concatenate([cache["fox_v"], v], axis=1)
        c = jnp.cumsum(jnp.concatenate([cache["fox_logf"].astype(jnp.float32), logf], axis=1), axis=1)
        o_a = _fox_core(q, k_all, v_all, c[:, P:], jnp.transpose(c, (0, 2, 1)), pos,
                        jnp.arange(P + T, dtype=jnp.int32))
        st_fox = (k, v, logf)
    o_a = o_a.reshape(B, T, WA)

    vn = _layernorm(b_v, lp["gm_ln_g"], lp["gm_ln_b"])
    wm = jnp.where(jnp.tril(jnp.ones((GM_CHUNK, GM_CHUNK), dtype=bool)), lp["gm_ws"], 0.0)
    if cache is None:
        n = T // GM_CHUNK
        vc = vn.reshape(B, n, GM_CHUNK, GB, GC)
        mix = jnp.einsum("gts,bnsgc->bntgc", wm, vc) + lp["gm_bs"].T[None, None, :, :, None]
        st_gm = vn[:, T - GM_CHUNK:]
    else:
        vc = vn.reshape(B, T, GB, GC)
        mix = jnp.einsum("gts,bsgc->btgc", wm[:, :T, :T], vc) + lp["gm_bs"][:, :T].T[None, :, :, None]
        st_gm = vn
    o_b = b_u * mix.reshape(B, T, WB)

    cq = _rmsnorm(c_q, lp["mla_q_norm_g"])
    qf = jnp.einsum("btr,rn->btn", cq, lp["mla_wq_b"]).reshape(B, T, HC, NOPE + ROPE_DIM)
    q_nope = qf[..., :NOPE]
    q_rope = _rope(qf[..., NOPE:], pos)
    ckv = _rmsnorm(c_kv, lp["mla_kv_norm_g"])
    kr = _rope(c_kr, pos)
    if cache is None:
        ckv_all, kr_all = ckv, kr
        kpos = jnp.arange(T, dtype=jnp.int32)
    else:
        P = cache["mla_ckv"].shape[1]
        ckv_all = jnp.concatenate([cache["mla_ckv"], ckv], axis=1)
        kr_all = jnp.concatenate([cache["mla_krope"], kr], axis=1)
        kpos = jnp.arange(P + T, dtype=jnp.int32)
    kv = jnp.einsum("bsr,rn->bsn", ckv_all, lp["mla_wkv_b"]).reshape(B, -1, HC, NOPE + VDIM)
    k_nope, v_c = kv[..., :NOPE], kv[..., NOPE:]
    if cache is None:
        o_c = _query_blocks(lambda qp, qn, qr: _mla_core(qn, qr, k_nope, v_c, kr_all, qp, kpos),
                            pos, q_nope, q_rope)
        st_mla = (ckv[:, T - keep:], kr[:, T - keep:])
    else:
        o_c = _mla_core(q_nope, q_rope, k_nope, v_c, kr_all, pos, kpos)
        st_mla = (ckv, kr)
    o_c = o_c.reshape(B, T, WC)

    cin = d_c * d_h
    prev = jnp.zeros((B, CONV_W - 1, WD), cin.dtype) if cache is None else cache["conv"].astype(cin.dtype)
    xp = jnp.concatenate([prev, cin], axis=1)
    conv = xp[:, 0:T] * lp["conv_w"][0]
    for j in range(1, CONV_W):
        conv = conv + xp[:, j:j + T] * lp["conv_w"][j]
    o_d = d_b * conv
    st_conv = xp[:, T:]

    gates = jax.nn.sigmoid(m_g.reshape(B, T, N_BRANCH, D_MODEL))
    branches = (o_a * jax.nn.silu(a_g), o_b * jax.nn.silu(b_g), o_c * jax.nn.silu(c_g), o_d * jax.nn.silu(d_g))
    merged = gates[:, :, 0] * jnp.einsum("btw,wd->btd", branches[0], lp["w_branch"][0])
    for nbr in range(1, N_BRANCH):
        merged = merged + gates[:, :, nbr] * jnp.einsum("btw,wd->btd", branches[nbr], lp["w_branch"][nbr])
    x = x + jnp.einsum("btd,de->bte", merged, lp["w_out"])
    return x, st_fox + st_mla + (st_conv, st_gm)


def setup_inputs(seed: int = 0) -> dict:
    key = jax.random.key(seed)
    ks = jax.random.split(key, 24)
    nrm = jax.random.normal
    f32 = jnp.float32
    return {
        "x_prompt": nrm(ks[0], (BATCH, SEQ, D_MODEL), f32),
        "x_sample": nrm(ks[1], (DEC_BATCH, DEC_SEQ, D_MODEL), f32),
        "cache_fox_k": nrm(ks[2], (DEPTH, DEC_BATCH, PAST_LEN, HA, DHA), f32),
        "cache_fox_v": nrm(ks[3], (DEPTH, DEC_BATCH, PAST_LEN, HA, DHA), f32),
        "cache_fox_logf": jax.nn.log_sigmoid(FORGET_BIAS + nrm(ks[4], (DEPTH, DEC_BATCH, PAST_LEN, HA), f32)),
        "cache_mla_ckv": nrm(ks[5], (DEPTH, DEC_BATCH, PAST_LEN, KV_RANK), f32),
        "cache_mla_krope": nrm(ks[6], (DEPTH, DEC_BATCH, PAST_LEN, ROPE_DIM), f32),
        "state_conv": nrm(ks[7], (DEPTH, DEC_BATCH, CONV_W - 1, WD), f32),
        "norm_g": 1.0 + 0.02 * nrm(ks[8], (DEPTH, D_MODEL), f32),
        "w_in": nrm(ks[9], (DEPTH, D_MODEL, N_IN), f32) * D_MODEL ** -0.5,
        "fox_fb": FORGET_BIAS + 0.1 * nrm(ks[10], (DEPTH, HA), f32),
        "gm_ln_g": 1.0 + 0.02 * nrm(ks[11], (DEPTH, WB), f32),
        "gm_ln_b": 0.02 * nrm(ks[12], (DEPTH, WB), f32),
        "gm_ws": nrm(ks[13], (DEPTH, GB, GM_CHUNK, GM_CHUNK), f32) * GM_CHUNK ** -0.5,
        "gm_bs": 1.0 + 0.02 * nrm(ks[14], (DEPTH, GB, GM_CHUNK), f32),
        "mla_q_norm_g": 1.0 + 0.02 * nrm(ks[15], (DEPTH, Q_RANK), f32),
        "mla_wq_b": nrm(ks[16], (DEPTH, Q_RANK, HC * (NOPE + ROPE_DIM)), f32) * Q_RANK ** -0.5,
        "mla_kv_norm_g": 1.0 + 0.02 * nrm(ks[17], (DEPTH, KV_RANK), f32),
        "mla_wkv_b": nrm(ks[18], (DEPTH, KV_RANK, HC * (NOPE + VDIM)), f32) * KV_RANK ** -0.5,
        "conv_w": nrm(ks[19], (DEPTH, CONV_W, WD), f32) * CONV_W ** -0.5,
        "w_branch": nrm(ks[20], (DEPTH, N_BRANCH, WA, D_MODEL), f32) * WA ** -0.5,
        "w_out": nrm(ks[21], (DEPTH, D_MODEL, D_MODEL), f32) * D_MODEL ** -0.5,
        "final_norm_g": 1.0 + 0.02 * nrm(ks[22], (D_MODEL,), f32),
    }


def reference(x_prompt, x_sample, cache_fox_k, cache_fox_v, cache_fox_logf, cache_mla_ckv, cache_mla_krope,
              state_conv, norm_g, w_in, fox_fb, gm_ln_g, gm_ln_b, gm_ws, gm_bs, mla_q_norm_g, mla_wq_b,
              mla_kv_norm_g, mla_wkv_b, conv_w, w_branch, w_out, final_norm_g):
    pos_p = jnp.arange(x_prompt.shape[1], dtype=jnp.int32)
    pos_s = cache_fox_k.shape[2] + jnp.arange(x_sample.shape[1], dtype=jnp.int32)
    hp, hs = x_prompt, x_sample
    sp, ss = [], []
    for l in range(DEPTH):
        lp = {"norm_g": norm_g[l], "w_in": w_in[l], "fox_fb": fox_fb[l], "gm_ln_g": gm_ln_g[l],
              "gm_ln_b": gm_ln_b[l], "gm_ws": gm_ws[l], "gm_bs": gm_bs[l], "mla_q_norm_g": mla_q_norm_g[l],
              "mla_wq_b": mla_wq_b[l], "mla_kv_norm_g": mla_kv_norm_g[l], "mla_wkv_b": mla_wkv_b[l],
              "conv_w": conv_w[l], "w_branch": w_branch[l], "w_out": w_out[l]}
        cache = {"fox_k": cache_fox_k[l], "fox_v": cache_fox_v[l], "fox_logf": cache_fox_logf[l],
                 "mla_ckv": cache_mla_ckv[l], "mla_krope": cache_mla_krope[l], "conv": state_conv[l]}
        hp, st_p = _layer(hp, pos_p, lp, None)
        hs, st_s = _layer(hs, pos_s, lp, cache)
        sp.append(st_p)
        ss.append(st_s)
    y_prompt = _rmsnorm(hp, final_norm_g)
    y_sample = _rmsnorm(hs, final_norm_g)
    p_fox_k, p_fox_v, p_fox_logf, p_mla_ckv, p_mla_krope, p_conv, p_gmlp_v = [
        jnp.stack([s[i] for s in sp]) for i in range(7)]
    s_fox_k, s_fox_v, s_fox_logf, s_mla_ckv, s_mla_krope, s_conv, s_gmlp_v = [
        jnp.stack([s[i] for s in ss]) for i in range(7)]
    return (y_prompt, y_sample, p_fox_k, p_fox_v, p_fox_logf, p_mla_ckv, p_mla_krope, p_conv, p_gmlp_v,
            s_fox_k, s_fox_v, s_fox_logf, s_mla_ckv, s_mla_krope, s_conv, s_gmlp_v)
```

```python
import functools

import jax
import jax.numpy as jnp
import numpy as np
from jax import lax
from jax.experimental import pallas as pl
from jax.experimental.pallas import tpu as pltpu

F32 = jnp.float32
BF16 = jnp.bfloat16

N_BRANCH = 4
HA, DHA = 4, 128
WA = HA * DHA
WB, GB = 512, 4
GC = WB // GB
GM_CHUNK = 128
HC, NOPE, ROPE_DIM, VDIM = 4, 128, 64, 128
Q_RANK, KV_RANK = 384, 256
WC = HC * VDIM
WD = 512
CONV_W = 3
CHUNK = 64
ROPE_BASE = 10000.0
EPS = 1e-6
NEG_INF = -1e30

LANE = 128
HEAD_PAD = 2 * LANE
VMEM_LIMIT = 56 * 1024 * 1024

A_Q, A_K, A_V, A_G, B_U, B_V, B_G, A_F = 0, 512, 1024, 1536, 2048, 2560, 3072, 3584
A_COLS = A_F + LANE
C_Q, C_KV, C_G, C_KRA, C_KRB = 0, 384, 640, 1152, 1280
D_B, D_C, D_H, D_G = 1408, 1920, 2432, 2944
B_COLS = D_G + WD


def _dot(a, b):
    return jnp.dot(a, b, preferred_element_type=F32)


def _silu(x):
    return x * jax.nn.sigmoid(x)


def _split3(x):
    hi = x.astype(BF16)
    r = x - hi.astype(F32)
    mid = r.astype(BF16)
    lo = (r - mid.astype(F32)).astype(BF16)
    return hi, mid, lo


def _const_spec(shape):
    nd = len(shape)
    return pl.BlockSpec(shape, lambda *_: (0,) * nd, pipeline_mode=pl.Buffered(1))


def _params(*sem):
    return pltpu.CompilerParams(dimension_semantics=sem, vmem_limit_bytes=VMEM_LIMIT)


def _rmsnorm_kernel(x_ref, g_ref, o_ref):
    x = x_ref[...]
    y = x * lax.rsqrt(jnp.mean(x * x, axis=-1, keepdims=True) + EPS)
    o_ref[...] = (y * g_ref[...]).astype(o_ref.dtype)


def _rmsnorm(x, g, tm, out_dtype):
    m, d = x.shape
    return pl.pallas_call(
        _rmsnorm_kernel,
        grid=(m // tm,),
        in_specs=[pl.BlockSpec((tm, d), lambda i: (i, 0)), _const_spec((1, d))],
        out_specs=pl.BlockSpec((tm, d), lambda i: (i, 0)),
        out_shape=jax.ShapeDtypeStruct((m, d), out_dtype),
        compiler_params=_params("parallel"),
        name="rmsnorm",
    )(x, g.reshape(1, d))


def _front_a_kernel(h_ref, w_ref, fb_ref, lng_ref, lnb_ref, wc_ref, bc_ref, tri_ref,
                    qa_ref, ka_ref, v_ref, ga_ref, k32_ref, v32_ref, logf_ref, cum_ref, ob_ref, vn_ref,
                    carry_ref, *, tiles_per_batch, chunk):
    i = pl.program_id(0)
    tm = h_ref.shape[0]
    h = h_ref[...]

    def seg(off, width=512):
        return _dot(h, w_ref[:, off:off + width])

    x = seg(A_F, LANE) + fb_ref[...]
    logf = jnp.minimum(x, 0.0) - jnp.log1p(jnp.exp(-jnp.abs(x)))
    logf_ref[...] = logf

    @pl.when(i % tiles_per_batch == 0)
    def _():
        carry_ref[...] = jnp.zeros_like(carry_ref)

    tri = tri_ref[...]
    hi, mid, lo = _split3(logf)
    cum = (_dot(tri, hi) + _dot(tri, mid)) + _dot(tri, lo) + carry_ref[0:1, :]
    cum_ref[...] = cum
    carry_ref[...] = jnp.broadcast_to(cum[tm - 1:tm, :], carry_ref.shape)

    zq = seg(A_Q) * (DHA ** -0.5)
    zk = seg(A_K)
    zv = seg(A_V)
    k32_ref[...] = zk
    v32_ref[...] = zv
    v_ref[...] = zv.astype(BF16)
    ga_ref[...] = _silu(seg(A_G)).astype(BF16)

    lane = lax.broadcasted_iota(jnp.int32, (tm, LANE), 1)
    c_hi = cum.astype(BF16).astype(F32)
    r = cum - c_hi
    c_mid = r.astype(BF16).astype(F32)
    c_lo = r - c_mid
    for hh in range(HA):
        bh = jnp.broadcast_to(c_hi[:, hh:hh + 1], (tm, LANE))
        bm = jnp.broadcast_to(c_mid[:, hh:hh + 1], (tm, LANE))
        bl = jnp.broadcast_to(c_lo[:, hh:hh + 1], (tm, LANE))
        aug_q = jnp.where(lane == 0, bh, jnp.where(lane == 1, bm, jnp.where(lane == 2, bl,
                          jnp.where(lane < 6, 1.0, 0.0))))
        aug_k = jnp.where(lane < 3, 1.0, jnp.where(lane == 3, -bh, jnp.where(lane == 4, -bm,
                          jnp.where(lane == 5, -bl, 0.0))))
        qa_ref[:, hh * HEAD_PAD:hh * HEAD_PAD + DHA] = zq[:, hh * DHA:(hh + 1) * DHA].astype(BF16)
        qa_ref[:, hh * HEAD_PAD + DHA:(hh + 1) * HEAD_PAD] = aug_q.astype(BF16)
        ka_ref[:, hh * HEAD_PAD:hh * HEAD_PAD + DHA] = zk[:, hh * DHA:(hh + 1) * DHA].astype(BF16)
        ka_ref[:, hh * HEAD_PAD + DHA:(hh + 1) * HEAD_PAD] = aug_k.astype(BF16)

    zbv = seg(B_V)
    mu = jnp.mean(zbv, axis=-1, keepdims=True)
    xc = zbv - mu
    var = jnp.mean(xc * xc, axis=-1, keepdims=True)
    vn = xc * lax.rsqrt(var + EPS) * lng_ref[...] + lnb_ref[...]
    vn_ref[...] = vn
    vnb = vn.astype(BF16)
    ug = seg(B_U) * _silu(seg(B_G))
    for ci in range(tm // chunk):
        rows = slice(ci * chunk, (ci + 1) * chunk)
        for g in range(GB):
            cols = slice(g * GC, (g + 1) * GC)
            mix = _dot(wc_ref[g], vnb[rows, cols]) + bc_ref[:, cols]
            ob_ref[rows, cols] = (ug[rows, cols] * mix).astype(BF16)


def _front_a(h, w, fb, lng, lnb, wc, bc, tri, *, tm, tiles_per_batch):
    m, d = h.shape
    chunk = wc.shape[1]
    row = lambda width: pl.BlockSpec((tm, width), lambda i: (i, 0))
    outs = [
        (4 * HEAD_PAD, BF16), (4 * HEAD_PAD, BF16), (WA, BF16), (WA, BF16), (WA, F32), (WA, F32),
        (LANE, F32), (LANE, F32), (WB, BF16), (WB, F32)]
    return pl.pallas_call(
        functools.partial(_front_a_kernel, tiles_per_batch=tiles_per_batch, chunk=chunk),
        grid=(m // tm,),
        in_specs=[row(d), _const_spec(w.shape), _const_spec((1, LANE)), _const_spec((1, WB)),
                  _const_spec((1, WB)), _const_spec(wc.shape), _const_spec(bc.shape), _const_spec(tri.shape)],
        out_specs=[row(wd) for wd, _ in outs],
        out_shape=[jax.ShapeDtypeStruct((m, wd), dt) for wd, dt in outs],
        scratch_shapes=[pltpu.VMEM((8, LANE), F32)],
        compiler_params=_params("arbitrary"),
        name="front_a",
    )(h, w, fb, lng, lnb, wc, bc, tri)


def _front_b_kernel(h_ref, w_ref, qg_ref, kvg_ref, wq_ref, wqs_ref, wkv_ref, cos_ref, sin_ref,
                    inj1_ref, inj2_ref, cw_ref,
                    qm_ref, km_ref, vc_ref, gc_ref, ckv32_ref, kr32_ref, od_ref, cin32_ref,
                    xp_ref, *, seq_len):
    i = pl.program_id(0)
    tm = h_ref.shape[0]
    h = h_ref[...]

    def seg(off, width=512):
        return _dot(h, w_ref[:, off:off + width])

    def rms(x, g):
        return x * lax.rsqrt(jnp.mean(x * x, axis=-1, keepdims=True) + EPS) * g

    cos = cos_ref[...]
    sin = sin_ref[...]
    scale = (NOPE + ROPE_DIM) ** -0.5

    cq = rms(seg(C_Q, Q_RANK), qg_ref[...]).astype(BF16)
    qf = _dot(cq, wq_ref[...])
    qs = _dot(cq, wqs_ref[...])
    for hh in range(HC):
        o = hh * HEAD_PAD
        qm_ref[:, o:o + NOPE] = (qf[:, o:o + NOPE] * scale).astype(BF16)
        rope = qf[:, o + NOPE:o + HEAD_PAD] * cos + qs[:, hh * LANE:(hh + 1) * LANE] * sin
        qm_ref[:, o + NOPE:o + HEAD_PAD] = (rope * scale).astype(BF16)

    ckv = rms(seg(C_KV, KV_RANK), kvg_ref[...])
    ckv32_ref[...] = ckv
    kv = _dot(ckv.astype(BF16), wkv_ref[...])
    kr = seg(C_KRA, LANE) * cos + seg(C_KRB, LANE) * sin
    kr32_ref[...] = kr
    krb = kr.astype(BF16)
    for hh in range(HC):
        o = hh * HEAD_PAD
        km_ref[:, o:o + NOPE] = kv[:, o:o + NOPE].astype(BF16)
        km_ref[:, o + NOPE:o + HEAD_PAD] = krb
        vc_ref[:, hh * VDIM:(hh + 1) * VDIM] = kv[:, o + NOPE:o + HEAD_PAD].astype(BF16)
    gc_ref[...] = _silu(seg(C_G)).astype(BF16)

    @pl.when(i == 0)
    def _():
        xp_ref[0:8, :] = jnp.zeros((8, WD), F32)

    cin = seg(D_C) * seg(D_H)
    cin32_ref[...] = cin
    xp_ref[8:8 + tm, :] = cin
    s1 = xp_ref[7:7 + tm, :]
    s2 = xp_ref[6:6 + tm, :]
    t_local = (i * tm + lax.broadcasted_iota(jnp.int32, (tm, WD), 0)) % seq_len
    s1 = jnp.where(t_local == 0, inj1_ref[...], s1)
    s2 = jnp.where(t_local < 2, inj2_ref[...], s2)
    conv = s2 * cw_ref[0:1, :] + s1 * cw_ref[1:2, :] + cin * cw_ref[2:3, :]
    od_ref[...] = (seg(D_B) * conv * _silu(seg(D_G))).astype(BF16)
    xp_ref[0:8, :] = xp_ref[tm:tm + 8, :]


def _front_b(h, w, qg, kvg, wq, wqs, wkv, cos, sin, inj1, inj2, cw, *, tm, seq_len):
    m, d = h.shape
    row = lambda width: pl.BlockSpec((tm, width), lambda i: (i, 0))
    n_pos = cos.shape[0] // tm
    n_inj = inj1.shape[0] // tm
    pos_spec = pl.BlockSpec((tm, LANE), lambda i: (i % n_pos, 0))
    inj_spec = pl.BlockSpec((tm, WD), lambda i: (i % n_inj, 0))
    outs = [(4 * HEAD_PAD, BF16), (4 * HEAD_PAD, BF16), (WC, BF16), (WC, BF16), (KV_RANK, F32),
            (LANE, F32), (WD, BF16), (WD, F32)]
    return pl.pallas_call(
        functools.partial(_front_b_kernel, seq_len=seq_len),
        grid=(m // tm,),
        in_specs=[row(d), _const_spec(w.shape), _const_spec((1, Q_RANK)), _const_spec((1, KV_RANK)),
                  _const_spec(wq.shape), _const_spec(wqs.shape), _const_spec(wkv.shape),
                  pos_spec, pos_spec, inj_spec, inj_spec, _const_spec(cw.shape)],
        out_specs=[row(wd) for wd, _ in outs],
        out_shape=[jax.ShapeDtypeStruct((m, wd), dt) for wd, dt in outs],
        scratch_shapes=[pltpu.VMEM((tm + 8, WD), F32)],
        compiler_params=_params("arbitrary"),
        name="front_b",
    )(h, w, qg, kvg, wq, wqs, wkv, cos, sin, inj1, inj2, cw)


def _flash_kernel(q_ref, k_ref, v_ref, g_ref, o_ref, m_ref, l_ref, acc_ref, *, t, gran):
    qi = pl.program_id(2)
    q = q_ref[...]
    m_ref[...] = jnp.full_like(m_ref, NEG_INF)
    l_ref[...] = jnp.zeros_like(l_ref)
    acc_ref[...] = jnp.zeros_like(acc_ref)

    def block(j, masked):
        start = pl.multiple_of(j * t, t)
        kb = k_ref[pl.ds(start, t), :]
        vb = v_ref[pl.ds(start, t), :]
        s = lax.dot_general(q, kb, (((1,), (1,)), ((), ())), preferred_element_type=F32)
        if masked:
            row = lax.broadcasted_iota(jnp.int32, (t, t), 0)
            col = lax.broadcasted_iota(jnp.int32, (t, t), 1)
            if gran > 1:
                row, col = row // gran, col // gran
            s = jnp.where(col <= row, s, NEG_INF)
        m_old = m_ref[...]
        m_new = jnp.maximum(m_old, jnp.max(s, axis=-1, keepdims=True))
        alpha = jnp.exp(m_old - m_new)
        p = jnp.exp(s - m_new)
        l_ref[...] = alpha * l_ref[...] + jnp.sum(p, axis=-1, keepdims=True)
        acc_ref[...] = alpha * acc_ref[...] + _dot(p.astype(BF16), vb)
        m_ref[...] = m_new

    def body(j, c):
        block(j, False)
        return c

    lax.fori_loop(0, qi, body, 0)
    block(qi, True)
    o_ref[...] = (acc_ref[...] / l_ref[...] * g_ref[...].astype(F32)).astype(o_ref.dtype)


def _flash(q, k, v, g, *, batch, seq_len, heads, t, gran):
    nq = seq_len // t
    dv = v.shape[1] // heads
    q_spec = pl.BlockSpec((t, HEAD_PAD), lambda b, h, i: (b * nq + i, h))
    o_spec = pl.BlockSpec((t, dv), lambda b, h, i: (b * nq + i, h))
    return pl.pallas_call(
        functools.partial(_flash_kernel, t=t, gran=gran),
        grid=(batch, heads, nq),
        in_specs=[q_spec,
                  pl.BlockSpec((seq_len, HEAD_PAD), lambda b, h, i: (b, h)),
                  pl.BlockSpec((seq_len, dv), lambda b, h, i: (b, h)),
                  o_spec],
        out_specs=o_spec,
        out_shape=jax.ShapeDtypeStruct(v.shape, BF16),
        scratch_shapes=[pltpu.VMEM((t, 1), F32), pltpu.VMEM((t, 1), F32), pltpu.VMEM((t, dv), F32)],
        compiler_params=_params("parallel", "parallel", "arbitrary"),
        name="flash_g%d" % gran,
    )(q, k, v, g)


def _sample_attn_kernel(qa_ref, ka_ref, v_ref, ga_ref, cumc_ref, cumr_ref, ck_ref, cv_ref, clogf_ref, utri_ref,
                        qm_ref, km_ref, vc_ref, gc_ref, cckv_ref, ckr_ref, wkv_ref,
                        oa_ref, oc_ref, *, past_len):
    ts = qa_ref.shape[0]
    row = lax.broadcasted_iota(jnp.int32, (ts, ts), 0)
    col = lax.broadcasted_iota(jnp.int32, (ts, ts), 1)
    nt = (((1,), (1,)), ((), ()))

    def attend(s_c, s_n, v_c, v_n):
        m = jnp.maximum(jnp.max(s_c, axis=-1, keepdims=True), jnp.max(s_n, axis=-1, keepdims=True))
        p_c = jnp.exp(s_c - m)
        p_n = jnp.exp(s_n - m)
        l = jnp.sum(p_c, axis=-1, keepdims=True) + jnp.sum(p_n, axis=-1, keepdims=True)
        return (_dot(p_c.astype(BF16), v_c) + _dot(p_n.astype(BF16), v_n)) / l

    utri = utri_ref[...]
    hi, mid, lo = _split3(clogf_ref[...])
    c_cache = (_dot(hi, utri) + _dot(mid, utri)) + _dot(lo, utri)
    cum_col = cumc_ref[...]
    cum_row = cumr_ref[...]
    causal = col <= row
    for hh in range(HA):
        q = qa_ref[:, hh * HEAD_PAD:hh * HEAD_PAD + DHA]
        kn = ka_ref[:, hh * HEAD_PAD:hh * HEAD_PAD + DHA]
        kc = ck_ref[:, hh * DHA:(hh + 1) * DHA].astype(BF16)
        vcache = cv_ref[:, hh * DHA:(hh + 1) * DHA].astype(BF16)
        base = c_cache[hh:hh + 1, past_len - 1:past_len]
        cq = cum_col[:, hh:hh + 1] + base
        s_c = lax.dot_general(q, kc, nt, preferred_element_type=F32) + (cq - c_cache[hh:hh + 1, :])
        s_n = lax.dot_general(q, kn, nt, preferred_element_type=F32) + (cq - (cum_row[hh:hh + 1, :] + base))
        s_n = jnp.where(causal, s_n, NEG_INF)
        o = attend(s_c, s_n, vcache, v_ref[:, hh * DHA:(hh + 1) * DHA])
        oa_ref[:, hh * DHA:(hh + 1) * DHA] = (o * ga_ref[:, hh * DHA:(hh + 1) * DHA].astype(F32)).astype(BF16)

    kv_c = _dot(cckv_ref[...].astype(BF16), wkv_ref[...])
    ckr = ckr_ref[...]
    chunk_ok = (past_len + col) // CHUNK <= (past_len + row) // CHUNK
    for hh in range(HC):
        o_ = hh * HEAD_PAD
        qn = qm_ref[:, o_:o_ + NOPE]
        qr = qm_ref[:, o_ + NOPE:o_ + HEAD_PAD]
        s_c = (lax.dot_general(qn, kv_c[:, o_:o_ + NOPE].astype(BF16), nt, preferred_element_type=F32)
               + lax.dot_general(qr, ckr, nt, preferred_element_type=F32))
        s_n = lax.dot_general(qm_ref[:, o_:o_ + HEAD_PAD], km_ref[:, o_:o_ + HEAD_PAD], nt,
                              preferred_element_type=F32)
        s_n = jnp.where(chunk_ok, s_n, NEG_INF)
        o = attend(s_c, s_n, kv_c[:, o_ + NOPE:o_ + HEAD_PAD].astype(BF16), vc_ref[:, hh * VDIM:(hh + 1) * VDIM])
        oc_ref[:, hh * VDIM:(hh + 1) * VDIM] = (o * gc_ref[:, hh * VDIM:(hh + 1) * VDIM].astype(F32)).astype(BF16)


def _sample_attn(qa, ka, v, ga, cum_col, cum_row, ck, cv, clogf, utri, qm, km, vc, gc, cckv, ckr, wkv,
                 *, batch, ts, past_len):
    row = lambda width: pl.BlockSpec((ts, width), lambda b: (b, 0))
    per_b = lambda shape: pl.BlockSpec((None,) + shape, lambda b: (b, 0, 0))
    return pl.pallas_call(
        functools.partial(_sample_attn_kernel, past_len=past_len),
        grid=(batch,),
        in_specs=[row(4 * HEAD_PAD), row(4 * HEAD_PAD), row(WA), row(WA), row(LANE), per_b((8, ts)),
                  per_b((past_len, WA)), per_b((past_len, WA)), per_b((8, past_len)), _const_spec(utri.shape),
                  row(4 * HEAD_PAD), row(4 * HEAD_PAD), row(WC), row(WC),
                  per_b((past_len, KV_RANK)), per_b((past_len, LANE)), _const_spec(wkv.shape)],
        out_specs=[row(WA), row(WC)],
        out_shape=[jax.ShapeDtypeStruct((batch * ts, WA), BF16), jax.ShapeDtypeStruct((batch * ts, WC), BF16)],
        compiler_params=_params("parallel"),
        name="sample_attn",
    )(qa, ka, v, ga, cum_col, cum_row, ck, cv, clogf, utri, qm, km, vc, gc, cckv, ckr, wkv)


def _merge_kernel(h_ref, oa_ref, ob_ref, oc_ref, od_ref, x_ref, wg_ref, wb_ref, wo_ref, gn_ref,
                  xo_ref, hn_ref):
    j = pl.program_id(1)

    @pl.when(j == 0)
    def _():
        xo_ref[...] = x_ref[...]

    h = h_ref[...]
    merged = None
    for br, o_ref in enumerate((oa_ref, ob_ref, oc_ref, od_ref)):
        term = jax.nn.sigmoid(_dot(h, wg_ref[br])) * _dot(o_ref[...], wb_ref[br])
        merged = term if merged is None else merged + term
    xo_ref[...] += _dot(merged.astype(BF16), wo_ref[...])

    @pl.when(j == pl.num_programs(1) - 1)
    def _():
        x = xo_ref[...]
        y = x * lax.rsqrt(jnp.mean(x * x, axis=-1, keepdims=True) + EPS)
        hn_ref[...] = (y * gn_ref[...]).astype(hn_ref.dtype)


def _merge(h, oa, ob, oc, od, x, wg, wb, wo, gn, *, tm, tj, norm_dtype):
    m, d = x.shape
    row = lambda width: pl.BlockSpec((tm, width), lambda i, j: (i, 0))
    return pl.pallas_call(
        _merge_kernel,
        grid=(m // tm, d // tj),
        in_specs=[row(d), row(WA), row(WB), row(WC), row(WD), row(d),
                  pl.BlockSpec((N_BRANCH, d, tj), lambda i, j: (0, 0, j)),
                  pl.BlockSpec((N_BRANCH, WA, tj), lambda i, j: (0, 0, j)),
                  pl.BlockSpec((tj, d), lambda i, j: (j, 0)),
                  pl.BlockSpec((1, d), lambda i, j: (0, 0))],
        out_specs=[row(d), row(d)],
        out_shape=[jax.ShapeDtypeStruct((m, d), F32), jax.ShapeDtypeStruct((m, d), norm_dtype)],
        compiler_params=_params("parallel", "arbitrary"),
        name="merge",
    )(h, oa, ob, oc, od, x, wg, wb, wo, gn)


def _rope_tables(pos):
    half = ROPE_DIM // 2
    inv = ROPE_BASE ** (-jnp.arange(half, dtype=F32) / half)
    ang = pos.astype(F32)[:, None] * inv[None, :]
    z = jnp.zeros((pos.shape[0], LANE - ROPE_DIM), F32)
    cos = jnp.concatenate([jnp.cos(ang), jnp.cos(ang), z], axis=-1)
    sin = jnp.concatenate([-jnp.sin(ang), jnp.sin(ang), z], axis=-1)
    return cos, sin


def _pad_cols(a, width):
    return jnp.pad(a, ((0, 0), (0, width - a.shape[1])))


def _layer_weights(l, d, w_in, fox_fb, gm_ln_g, gm_ln_b, mla_q_norm_g, mla_wq_b, mla_kv_norm_g, mla_wkv_b,
                   conv_w, w_branch, w_out):
    sizes = (WA, WA, WA, HA, WA, WB, WB, WB, Q_RANK, KV_RANK, ROPE_DIM, WC, WD, WD, WD, WD, N_BRANCH * d)
    offs = np.concatenate([[0], np.cumsum(sizes)])
    w = w_in[l]
    (a_q, a_k, a_v, a_f, a_g, b_u, b_v, b_g, c_q, c_kv, c_kr, c_g, d_b, d_c, d_h, d_g, m_g) = [
        w[:, int(offs[n]):int(offs[n + 1])] for n in range(len(sizes))]
    half = ROPE_DIM // 2
    swap = lambda a: jnp.concatenate([a[..., half:], a[..., :half]], axis=-1)
    wa = jnp.concatenate([a_q, a_k, a_v, a_g, b_u, b_v, b_g, _pad_cols(a_f, LANE)], axis=1).astype(BF16)
    wb2 = jnp.concatenate([c_q, c_kv, c_g, _pad_cols(c_kr, LANE), _pad_cols(swap(c_kr), LANE),
                           d_b, d_c, d_h, d_g], axis=1).astype(BF16)
    wq = mla_wq_b[l].reshape(Q_RANK, HC, NOPE + ROPE_DIM)
    zpad = jnp.zeros((Q_RANK, HC, LANE - ROPE_DIM), F32)
    wq2 = jnp.concatenate([wq, zpad], axis=-1).reshape(Q_RANK, HC * HEAD_PAD).astype(BF16)
    wqs = jnp.concatenate([swap(wq[..., NOPE:]), zpad], axis=-1).reshape(Q_RANK, HC * LANE).astype(BF16)
    return dict(
        wa=wa, wb2=wb2, wq=wq2, wqs=wqs, wkv=mla_wkv_b[l].astype(BF16),
        fb=_pad_cols(fox_fb[l].reshape(1, HA), LANE),
        lng=gm_ln_g[l].reshape(1, WB), lnb=gm_ln_b[l].reshape(1, WB),
        qg=mla_q_norm_g[l].reshape(1, Q_RANK), kvg=mla_kv_norm_g[l].reshape(1, KV_RANK),
        cw=jnp.pad(conv_w[l], ((0, 8 - CONV_W), (0, 0))),
        wg=m_g.reshape(d, N_BRANCH, d).transpose(1, 0, 2).astype(BF16),
        wbr=w_branch[l].astype(BF16), wo=w_out[l].astype(BF16))


def _group_layer(lw, x, h, gn, norm_dtype, *, batch, seq_len, tm, cos, sin, inj1, inj2, wc, bc, tri, attn):
    tiles_per_batch = max(seq_len // tm, 1)
    fa = _front_a(h, lw["wa"], lw["fb"], lw["lng"], lw["lnb"], wc, bc, tri, tm=tm, tiles_per_batch=tiles_per_batch)
    fb = _front_b(h, lw["wb2"], lw["qg"], lw["kvg"], lw["wq"], lw["wqs"], lw["wkv"], cos, sin, inj1, inj2,
                  lw["cw"], tm=tm, seq_len=seq_len)
    qa, ka, v, ga, k32, v32, logf, cum, ob, vn = fa
    qm, km, vc, gc, ckv32, kr32, od, cin32 = fb
    oa, oc = attn(fa, fb)
    d = x.shape[1]
    tj = min(256, d)
    x_new, normed = _merge(h, oa, ob, oc, od, x, lw["wg"], lw["wbr"], lw["wo"], gn.reshape(1, d),
                           tm=tm, tj=tj, norm_dtype=norm_dtype)
    r3 = lambda a: a.reshape(batch, seq_len, a.shape[-1])
    states = dict(k=r3(k32), v=r3(v32), logf=r3(logf)[..., :HA], ckv=r3(ckv32), kr=r3(kr32)[..., :ROPE_DIM],
                  cin=r3(cin32), vn=r3(vn))
    return x_new, normed, states


def kernel(x_prompt, x_sample, cache_fox_k, cache_fox_v, cache_fox_logf, cache_mla_ckv, cache_mla_krope, state_conv, norm_g, w_in, fox_fb, gm_ln_g, gm_ln_b, gm_ws, gm_bs, mla_q_norm_g, mla_wq_b, mla_kv_norm_g, mla_wkv_b, conv_w, w_branch, w_out, final_norm_g):
    bp, tp, d = x_prompt.shape
    bs, ts, _ = x_sample.shape
    depth = w_in.shape[0]
    past = cache_fox_k.shape[2]
    keep = min(tp, past)
    tm_p = min(512, tp)
    tm_s = bs * ts
    t_attn = min(512, tp)
    assert tp % tm_p == 0 and tm_p % GM_CHUNK == 0 and tp % t_attn == 0 and t_attn % CHUNK == 0
    assert ts <= GM_CHUNK and tm_s % 8 == 0 and past % CHUNK == 0

    xp = x_prompt.reshape(bp * tp, d)
    xs = x_sample.reshape(bs * ts, d)

    cos_p, sin_p = _rope_tables(jnp.arange(tp, dtype=jnp.int32))
    cos_s, sin_s = _rope_tables(past + jnp.arange(ts, dtype=jnp.int32))
    cos_s, sin_s = jnp.tile(cos_s, (bs, 1)), jnp.tile(sin_s, (bs, 1))
    ar_p = np.arange(tm_p)
    tri_p = jnp.asarray(ar_p[None, :] <= ar_p[:, None], BF16)
    ar_s = np.arange(tm_s)
    same_b = (ar_s[None, :] // ts) == (ar_s[:, None] // ts)
    tri_s = jnp.asarray(same_b & (ar_s[None, :] <= ar_s[:, None]), BF16)
    ar_c = np.arange(past)
    utri_c = jnp.asarray(ar_c[:, None] <= ar_c[None, :], BF16)
    zero_inj = jnp.zeros((tm_p, WD), F32)
    tril = np.tril(np.ones((GM_CHUNK, GM_CHUNK), bool))

    hp = _rmsnorm(xp, norm_g[0], tm_p, BF16)
    hs = _rmsnorm(xs, norm_g[0], tm_s, BF16)
    st_p, st_s = [], []
    for l in range(depth):
        lw = _layer_weights(l, d, w_in, fox_fb, gm_ln_g, gm_ln_b, mla_q_norm_g, mla_wq_b, mla_kv_norm_g,
                            mla_wkv_b, conv_w, w_branch, w_out)
        last = l == depth - 1
        gn = final_norm_g if last else norm_g[l + 1]
        norm_dtype = F32 if last else BF16

        wm = jnp.where(tril, gm_ws[l], 0.0)
        wc_p = wm.astype(BF16)
        bc_p = jnp.repeat(gm_bs[l].T, GC, axis=1)
        wc_s = jnp.einsum("ab,gts->gatbs", jnp.eye(bs, dtype=F32), wm[:, :ts, :ts]).reshape(
            GB, tm_s, tm_s).astype(BF16)
        bc_s = jnp.tile(bc_p[:ts], (bs, 1))

        def attn_p(fa, fb):
            oa = _flash(fa[0], fa[1], fa[2], fa[3], batch=bp, seq_len=tp, heads=HA, t=t_attn, gran=1)
            oc = _flash(fb[0], fb[1], fb[2], fb[3], batch=bp, seq_len=tp, heads=HC, t=t_attn, gran=CHUNK)
            return oa, oc

        xp, hp, sp = _group_layer(lw, xp, hp, gn, norm_dtype, batch=bp, seq_len=tp, tm=tm_p, cos=cos_p, sin=sin_p,
                                  inj1=zero_inj, inj2=zero_inj, wc=wc_p, bc=bc_p, tri=tri_p, attn=attn_p)

        sc = state_conv[l]
        inj1 = jnp.zeros((bs, ts, WD), F32).at[:, 0].set(sc[:, 1]).reshape(tm_s, WD)
        inj2 = jnp.zeros((bs, ts, WD), F32).at[:, 0].set(sc[:, 0]).at[:, 1].set(sc[:, 1]).reshape(tm_s, WD)
        clogf = jnp.pad(jnp.transpose(cache_fox_logf[l], (0, 2, 1)), ((0, 0), (0, 8 - HA), (0, 0)))
        ckr = jnp.pad(cache_mla_krope[l], ((0, 0), (0, 0), (0, LANE - ROPE_DIM))).astype(BF16)

        def attn_s(fa, fb):
            cum_row = jnp.transpose(fa[7].reshape(bs, ts, LANE)[:, :, :8], (0, 2, 1))
            return _sample_attn(fa[0], fa[1], fa[2], fa[3], fa[7], cum_row,
                                cache_fox_k[l].reshape(bs, past, WA), cache_fox_v[l].reshape(bs, past, WA),
                                clogf, utri_c, fb[0], fb[1], fb[2], fb[3], cache_mla_ckv[l], ckr, lw["wkv"],
                                batch=bs, ts=ts, past_len=past)

        xs, hs, ss = _group_layer(lw, xs, hs, gn, norm_dtype, batch=bs, seq_len=ts, tm=tm_s, cos=cos_s, sin=sin_s,
                                  inj1=inj1, inj2=inj2, wc=wc_s, bc=bc_s, tri=tri_s, attn=attn_s)
        st_p.append(sp)
        st_s.append(ss)

    y_prompt = hp.reshape(bp, tp, d)
    y_sample = hs.reshape(bs, ts, d)
    stk = lambda sts, f: jnp.stack([f(s) for s in sts])
    p_out = (stk(st_p, lambda s: s["k"][:, tp - keep:].reshape(bp, keep, HA, DHA)),
             stk(st_p, lambda s: s["v"][:, tp - keep:].reshape(bp, keep, HA, DHA)),
             stk(st_p, lambda s: s["logf"][:, tp - keep:]),
             stk(st_p, lambda s: s["ckv"][:, tp - keep:]),
             stk(st_p, lambda s: s["kr"][:, tp - keep:]),
             stk(st_p, lambda s: s["cin"][:, tp - (CONV_W - 1):]),
             stk(st_p, lambda s: s["vn"][:, tp - GM_CHUNK:]))
    s_out = (stk(st_s, lambda s: s["k"].reshape(bs, ts, HA, DHA)),
             stk(st_s, lambda s: s["v"].reshape(bs, ts, HA, DHA)),
             stk(st_s, lambda s: s["logf"]),
             stk(st_s, lambda s: s["ckv"]),
             stk(st_s, lambda s: s["kr"]),
             stk(st_s, lambda s: s["cin"][:, ts - (CONV_W - 1):]),
             stk(st_s, lambda s: s["vn"]))
    return (y_prompt, y_sample) + p_out + s_out
```

```python
import functools

import jax
import jax.numpy as jnp
import numpy as np
from jax import lax
from jax.experimental import pallas as pl
from jax.experimental.pallas import tpu as pltpu

F32 = jnp.float32
BF16 = jnp.bfloat16

N_BRANCH = 4
HA, DHA = 4, 128
WA = HA * DHA
WB, GB = 512, 4
GC = WB // GB
GM_CHUNK = 128
HC, NOPE, ROPE_DIM, VDIM = 4, 128, 64, 128
Q_RANK, KV_RANK = 384, 256
WC = HC * VDIM
WD = 512
CONV_W = 3
CHUNK = 64
ROPE_BASE = 10000.0
EPS = 1e-6
NEG_INF = -1e30

LOG2E = 1.4426950408889634
LANE = 128
HEAD_PAD = 2 * LANE
DV_PAD = VDIM + 16
VMEM_LIMIT = 56 * 1024 * 1024

A_Q, A_K, A_V, A_G, B_U, B_V, B_G, A_F = 0, 512, 1024, 1536, 2048, 2560, 3072, 3584
A_COLS = A_F + LANE
C_Q, C_KV, C_G, C_KRA, C_KRB = 0, 384, 640, 1152, 1280
D_B, D_C, D_H, D_G = 1408, 1920, 2432, 2944
B_COLS = D_G + WD


def _dot(a, b):
    return jnp.dot(a, b, preferred_element_type=F32)


def _silu(x):
    return x * jax.nn.sigmoid(x)


def _split3(x):
    hi = x.astype(BF16)
    r = x - hi.astype(F32)
    mid = r.astype(BF16)
    lo = (r - mid.astype(F32)).astype(BF16)
    return hi, mid, lo


def _store_values_transposed(vt_ref, v):
    rows = v.shape[0]
    sub = lax.broadcasted_iota(jnp.int32, (DV_PAD - VDIM, rows), 0)
    ones_row = jnp.where(sub == 0, 1.0, 0.0).astype(BF16)
    for hh in range(v.shape[1] // VDIM):
        vt_ref[0, hh, 0:VDIM, :] = v[:, hh * VDIM:(hh + 1) * VDIM].T.astype(BF16)
        vt_ref[0, hh, VDIM:DV_PAD, :] = ones_row


def _vt_specs(m, tm, heads):
    return (pl.BlockSpec((1, heads, DV_PAD, tm), lambda i: (i, 0, 0, 0)),
            jax.ShapeDtypeStruct((m // tm, heads, DV_PAD, tm), BF16))


def _const_spec(shape):
    nd = len(shape)
    return pl.BlockSpec(shape, lambda *_: (0,) * nd, pipeline_mode=pl.Buffered(1))


def _params(*sem):
    return pltpu.CompilerParams(dimension_semantics=sem, vmem_limit_bytes=VMEM_LIMIT)


def _rmsnorm_kernel(x_ref, g_ref, o_ref):
    x = x_ref[...]
    y = x * lax.rsqrt(jnp.mean(x * x, axis=-1, keepdims=True) + EPS)
    o_ref[...] = (y * g_ref[...]).astype(o_ref.dtype)


def _rmsnorm(x, g, tm, out_dtype):
    m, d = x.shape
    return pl.pallas_call(
        _rmsnorm_kernel,
        grid=(m // tm,),
        in_specs=[pl.BlockSpec((tm, d), lambda i: (i, 0)), _const_spec((1, d))],
        out_specs=pl.BlockSpec((tm, d), lambda i: (i, 0)),
        out_shape=jax.ShapeDtypeStruct((m, d), out_dtype),
        compiler_params=_params("parallel"),
        name="rmsnorm",
    )(x, g.reshape(1, d))


def _front_a_kernel(h_ref, w_ref, fb_ref, lng_ref, lnb_ref, wc_ref, bc_ref, tri_ref,
                    qa_ref, ka_ref, v_ref, ga_ref, k32_ref, v32_ref, logf_ref, cum_ref, ob_ref, vn_ref, vt_ref,
                    carry_ref, *, tiles_per_batch, chunk):
    i = pl.program_id(0)
    tm = h_ref.shape[0]
    h = h_ref[...]

    def seg(off, width=512):
        return _dot(h, w_ref[:, off:off + width])

    x = seg(A_F, LANE) + fb_ref[...]
    logf = jnp.minimum(x, 0.0) - jnp.log1p(jnp.exp(-jnp.abs(x)))
    logf_ref[...] = logf

    @pl.when(i % tiles_per_batch == 0)
    def _():
        carry_ref[...] = jnp.zeros_like(carry_ref)

    tri = tri_ref[...]
    hi, mid, lo = _split3(logf)
    cum = (_dot(tri, hi) + _dot(tri, mid)) + _dot(tri, lo) + carry_ref[0:1, :]
    carry_ref[...] = jnp.broadcast_to(cum[tm - 1:tm, :], carry_ref.shape)
    cum = cum * LOG2E
    cum_ref[...] = cum

    zq = seg(A_Q) * (DHA ** -0.5 * LOG2E)
    zk = seg(A_K)
    zv = seg(A_V)
    k32_ref[...] = zk
    v32_ref[...] = zv
    v_ref[...] = zv.astype(BF16)
    _store_values_transposed(vt_ref, zv)
    ga_ref[...] = _silu(seg(A_G)).astype(BF16)

    lane = lax.broadcasted_iota(jnp.int32, (tm, LANE), 1)
    c_hi = cum.astype(BF16).astype(F32)
    r = cum - c_hi
    c_mid = r.astype(BF16).astype(F32)
    c_lo = r - c_mid
    for hh in range(HA):
        bh = jnp.broadcast_to(c_hi[:, hh:hh + 1], (tm, LANE))
        bm = jnp.broadcast_to(c_mid[:, hh:hh + 1], (tm, LANE))
        bl = jnp.broadcast_to(c_lo[:, hh:hh + 1], (tm, LANE))
        aug_q = jnp.where(lane == 0, bh, jnp.where(lane == 1, bm, jnp.where(lane == 2, bl,
                          jnp.where(lane < 6, 1.0, 0.0))))
        aug_k = jnp.where(lane < 3, 1.0, jnp.where(lane == 3, -bh, jnp.where(lane == 4, -bm,
                          jnp.where(lane == 5, -bl, 0.0))))
        qa_ref[:, hh * HEAD_PAD:hh * HEAD_PAD + DHA] = zq[:, hh * DHA:(hh + 1) * DHA].astype(BF16)
        qa_ref[:, hh * HEAD_PAD + DHA:(hh + 1) * HEAD_PAD] = aug_q.astype(BF16)
        ka_ref[:, hh * HEAD_PAD:hh * HEAD_PAD + DHA] = zk[:, hh * DHA:(hh + 1) * DHA].astype(BF16)
        ka_ref[:, hh * HEAD_PAD + DHA:(hh + 1) * HEAD_PAD] = aug_k.astype(BF16)

    zbv = seg(B_V)
    mu = jnp.mean(zbv, axis=-1, keepdims=True)
    xc = zbv - mu
    var = jnp.mean(xc * xc, axis=-1, keepdims=True)
    vn = xc * lax.rsqrt(var + EPS) * lng_ref[...] + lnb_ref[...]
    vn_ref[...] = vn
    vnb = vn.astype(BF16)
    ug = seg(B_U) * _silu(seg(B_G))
    for ci in range(tm // chunk):
        rows = slice(ci * chunk, (ci + 1) * chunk)
        for g in range(GB):
            cols = slice(g * GC, (g + 1) * GC)
            mix = _dot(wc_ref[g], vnb[rows, cols]) + bc_ref[:, cols]
            ob_ref[rows, cols] = (ug[rows, cols] * mix).astype(BF16)


def _front_a(h, w, fb, lng, lnb, wc, bc, tri, *, tm, tiles_per_batch):
    m, d = h.shape
    chunk = wc.shape[1]
    row = lambda width: pl.BlockSpec((tm, width), lambda i: (i, 0))
    outs = [
        (4 * HEAD_PAD, BF16), (4 * HEAD_PAD, BF16), (WA, BF16), (WA, BF16), (WA, F32), (WA, F32),
        (LANE, F32), (LANE, F32), (WB, BF16), (WB, F32)]
    vt_spec, vt_shape = _vt_specs(m, tm, HA)
    return pl.pallas_call(
        functools.partial(_front_a_kernel, tiles_per_batch=tiles_per_batch, chunk=chunk),
        grid=(m // tm,),
        in_specs=[row(d), _const_spec(w.shape), _const_spec((1, LANE)), _const_spec((1, WB)),
                  _const_spec((1, WB)), _const_spec(wc.shape), _const_spec(bc.shape), _const_spec(tri.shape)],
        out_specs=[row(wd) for wd, _ in outs] + [vt_spec],
        out_shape=[jax.ShapeDtypeStruct((m, wd), dt) for wd, dt in outs] + [vt_shape],
        scratch_shapes=[pltpu.VMEM((8, LANE), F32)],
        compiler_params=_params("arbitrary"),
        name="front_a",
    )(h, w, fb, lng, lnb, wc, bc, tri)


def _front_b_kernel(h_ref, w_ref, qg_ref, kvg_ref, wq_ref, wqs_ref, wkv_ref, cos_ref, sin_ref,
                    inj1_ref, inj2_ref, cw_ref,
                    qm_ref, km_ref, vc_ref, gc_ref, ckv32_ref, kr32_ref, od_ref, cin32_ref, vt_ref,
                    xp_ref, *, seq_len):
    i = pl.program_id(0)
    tm = h_ref.shape[0]
    h = h_ref[...]

    def seg(off, width=512):
        return _dot(h, w_ref[:, off:off + width])

    def rms(x, g):
        return x * lax.rsqrt(jnp.mean(x * x, axis=-1, keepdims=True) + EPS) * g

    cos = cos_ref[...]
    sin = sin_ref[...]
    scale = (NOPE + ROPE_DIM) ** -0.5 * LOG2E

    cq = rms(seg(C_Q, Q_RANK), qg_ref[...]).astype(BF16)
    qf = _dot(cq, wq_ref[...])
    qs = _dot(cq, wqs_ref[...])
    for hh in range(HC):
        o = hh * HEAD_PAD
        qm_ref[:, o:o + NOPE] = (qf[:, o:o + NOPE] * scale).astype(BF16)
        rope = qf[:, o + NOPE:o + HEAD_PAD] * cos + qs[:, hh * LANE:(hh + 1) * LANE] * sin
        qm_ref[:, o + NOPE:o + HEAD_PAD] = (rope * scale).astype(BF16)

    ckv = rms(seg(C_KV, KV_RANK), kvg_ref[...])
    ckv32_ref[...] = ckv
    kv = _dot(ckv.astype(BF16), wkv_ref[...])
    kr = seg(C_KRA, LANE) * cos + seg(C_KRB, LANE) * sin
    kr32_ref[...] = kr
    krb = kr.astype(BF16)
    for hh in range(HC):
        o = hh * HEAD_PAD
        km_ref[:, o:o + NOPE] = kv[:, o:o + NOPE].astype(BF16)
        km_ref[:, o + NOPE:o + HEAD_PAD] = krb
    vcs = jnp.concatenate([kv[:, hh * HEAD_PAD + NOPE:(hh + 1) * HEAD_PAD] for hh in range(HC)], axis=1)
    vc_ref[...] = vcs.astype(BF16)
    _store_values_transposed(vt_ref, vcs)
    gc_ref[...] = _silu(seg(C_G)).astype(BF16)

    @pl.when(i == 0)
    def _():
        xp_ref[0:8, :] = jnp.zeros((8, WD), F32)

    cin = seg(D_C) * seg(D_H)
    cin32_ref[...] = cin
    xp_ref[8:8 + tm, :] = cin
    s1 = xp_ref[7:7 + tm, :]
    s2 = xp_ref[6:6 + tm, :]
    t_local = (i * tm + lax.broadcasted_iota(jnp.int32, (tm, WD), 0)) % seq_len
    s1 = jnp.where(t_local == 0, inj1_ref[...], s1)
    s2 = jnp.where(t_local < 2, inj2_ref[...], s2)
    conv = s2 * cw_ref[0:1, :] + s1 * cw_ref[1:2, :] + cin * cw_ref[2:3, :]
    od_ref[...] = (seg(D_B) * conv * _silu(seg(D_G))).astype(BF16)
    xp_ref[0:8, :] = xp_ref[tm:tm + 8, :]


def _front_b(h, w, qg, kvg, wq, wqs, wkv, cos, sin, inj1, inj2, cw, *, tm, seq_len):
    m, d = h.shape
    row = lambda width: pl.BlockSpec((tm, width), lambda i: (i, 0))
    n_pos = cos.shape[0] // tm
    n_inj = inj1.shape[0] // tm
    pos_spec = pl.BlockSpec((tm, LANE), lambda i: (i % n_pos, 0))
    inj_spec = pl.BlockSpec((tm, WD), lambda i: (i % n_inj, 0))
    outs = [(4 * HEAD_PAD, BF16), (4 * HEAD_PAD, BF16), (WC, BF16), (WC, BF16), (KV_RANK, F32),
            (LANE, F32), (WD, BF16), (WD, F32)]
    vt_spec, vt_shape = _vt_specs(m, tm, HC)
    return pl.pallas_call(
        functools.partial(_front_b_kernel, seq_len=seq_len),
        grid=(m // tm,),
        in_specs=[row(d), _const_spec(w.shape), _const_spec((1, Q_RANK)), _const_spec((1, KV_RANK)),
                  _const_spec(wq.shape), _const_spec(wqs.shape), _const_spec(wkv.shape),
                  pos_spec, pos_spec, inj_spec, inj_spec, _const_spec(cw.shape)],
        out_specs=[row(wd) for wd, _ in outs] + [vt_spec],
        out_shape=[jax.ShapeDtypeStruct((m, wd), dt) for wd, dt in outs] + [vt_shape],
        scratch_shapes=[pltpu.VMEM((tm + 8, WD), F32)],
        compiler_params=_params("arbitrary"),
        name="front_b",
    )(h, w, qg, kvg, wq, wqs, wkv, cos, sin, inj1, inj2, cw)


FLASH_HEADS = 2


def _flash_kernel(q_ref, k_ref, vt_ref, g_ref, bias_ref, o_ref,
                  s_ref, p_ref, pm_ref, a_ref, m_ref, acc_ref, *, t, hp):
    qi = pl.program_id(2)
    nk = k_ref.shape[0] // t
    nt = (((1,), (1,)), ((), ()))

    def qk(j, slot):
        jc = jnp.minimum(j, nk - 1)
        sel = jnp.clip(j - qi + 1, 0, 2)
        for h in range(hp):
            kb = k_ref[pl.ds(pl.multiple_of(jc * t, t), t), h * HEAD_PAD:(h + 1) * HEAD_PAD]
            q = q_ref[:, h * HEAD_PAD:(h + 1) * HEAD_PAD]
            s = lax.dot_general(kb, q, nt, preferred_element_type=F32) + bias_ref[sel]
            s_ref[slot, h] = s
            pm = s[0:8, :]
            for c in range(1, t // 8):
                pm = jnp.maximum(pm, s[c * 8:(c + 1) * 8, :])
            pm_ref[slot, h] = pm

    def softmax(slot):
        for h in range(hp):
            m_old = m_ref[h]
            m_new = jnp.maximum(m_old, jnp.max(pm_ref[slot, h], axis=0, keepdims=True))
            a_ref[slot, h] = jnp.exp2(m_old - m_new)
            m_ref[h] = m_new
            p_ref[slot, h] = jnp.exp2(s_ref[slot, h] - m_new).astype(BF16)

    def pv(j, slot):
        jc = jnp.clip(j, 0, nk - 1)
        for h in range(hp):
            acc_ref[h] = a_ref[slot, h] * acc_ref[h] + _dot(vt_ref[jc, h], p_ref[slot, h])

    m_ref[...] = jnp.full_like(m_ref, NEG_INF)
    acc_ref[...] = jnp.zeros_like(acc_ref)
    p_ref[1] = jnp.zeros_like(p_ref[1])
    a_ref[1] = jnp.ones_like(a_ref[1])
    qk(0, 0)

    def pair(jj, c):
        j = 2 * jj
        qk(j + 1, 1)
        softmax(0)
        pv(j - 1, 1)
        qk(j + 2, 0)
        softmax(1)
        pv(j, 0)
        return c

    n_pairs = qi // 2 + 1
    lax.fori_loop(0, n_pairs, pair, 0)
    pv(2 * n_pairs - 1, 1)
    for h in range(hp):
        acc = acc_ref[h]
        o = (acc[:VDIM, :] / acc[VDIM:VDIM + 1, :]).T
        cols = slice(h * VDIM, (h + 1) * VDIM)
        o_ref[:, cols] = (o * g_ref[:, cols].astype(F32)).astype(o_ref.dtype)


def _flash(q, k, vt, g, bias, *, batch, seq_len, heads, t, name):
    nq = seq_len // t
    hp = FLASH_HEADS
    q_spec = pl.BlockSpec((t, hp * HEAD_PAD), lambda b, h, i: (b * nq + i, h))
    o_spec = pl.BlockSpec((t, hp * VDIM), lambda b, h, i: (b * nq + i, h))
    once = pl.Buffered(1)
    return pl.pallas_call(
        functools.partial(_flash_kernel, t=t, hp=hp),
        grid=(batch, heads // hp, nq),
        in_specs=[q_spec,
                  pl.BlockSpec((seq_len, hp * HEAD_PAD), lambda b, h, i: (b, h), pipeline_mode=once),
                  pl.BlockSpec((nq, hp, DV_PAD, t), lambda b, h, i: (b, h, 0, 0), pipeline_mode=once),
                  o_spec,
                  _const_spec(bias.shape)],
        out_specs=o_spec,
        out_shape=jax.ShapeDtypeStruct(g.shape, BF16),
        scratch_shapes=[pltpu.VMEM((2, hp, t, t), F32), pltpu.VMEM((2, hp, t, t), BF16),
                        pltpu.VMEM((2, hp, 8, t), F32), pltpu.VMEM((2, hp, 1, t), F32),
                        pltpu.VMEM((hp, 1, t), F32), pltpu.VMEM((hp, DV_PAD, t), F32)],
        compiler_params=_params("parallel", "parallel", "arbitrary"),
        name=name,
    )(q, k, vt, g, bias)


def _mask_bias(t, gran):
    key = np.arange(t)[:, None] // gran
    qry = np.arange(t)[None, :] // gran
    diag = np.where(key <= qry, 0.0, NEG_INF)
    return jnp.asarray(np.stack([np.zeros((t, t)), diag, np.full((t, t), NEG_INF)]), F32)


def _sample_attn_kernel(qa_ref, ka_ref, v_ref, ga_ref, cumc_ref, cumr_ref, ck_ref, cv_ref, clogf_ref, utri_ref,
                        qm_ref, km_ref, vc_ref, gc_ref, cckv_ref, ckr_ref, wkv_ref,
                        oa_ref, oc_ref, *, past_len):
    ts = qa_ref.shape[0]
    row = lax.broadcasted_iota(jnp.int32, (ts, ts), 0)
    col = lax.broadcasted_iota(jnp.int32, (ts, ts), 1)
    nt = (((1,), (1,)), ((), ()))

    def attend(s_c, s_n, v_c, v_n):
        m = jnp.maximum(jnp.max(s_c, axis=-1, keepdims=True), jnp.max(s_n, axis=-1, keepdims=True))
        p_c = jnp.exp2(s_c - m)
        p_n = jnp.exp2(s_n - m)
        l = jnp.sum(p_c, axis=-1, keepdims=True) + jnp.sum(p_n, axis=-1, keepdims=True)
        return (_dot(p_c.astype(BF16), v_c) + _dot(p_n.astype(BF16), v_n)) / l

    utri = utri_ref[...]
    hi, mid, lo = _split3(clogf_ref[...])
    c_cache = ((_dot(hi, utri) + _dot(mid, utri)) + _dot(lo, utri)) * LOG2E
    cum_col = cumc_ref[...]
    cum_row = cumr_ref[...]
    causal = col <= row
    for hh in range(HA):
        q = qa_ref[:, hh * HEAD_PAD:hh * HEAD_PAD + DHA]
        kn = ka_ref[:, hh * HEAD_PAD:hh * HEAD_PAD + DHA]
        kc = ck_ref[:, hh * DHA:(hh + 1) * DHA].astype(BF16)
        vcache = cv_ref[:, hh * DHA:(hh + 1) * DHA].astype(BF16)
        base = c_cache[hh:hh + 1, past_len - 1:past_len]
        cq = cum_col[:, hh:hh + 1] + base
        s_c = lax.dot_general(q, kc, nt, preferred_element_type=F32) + (cq - c_cache[hh:hh + 1, :])
        s_n = lax.dot_general(q, kn, nt, preferred_element_type=F32) + (cq - (cum_row[hh:hh + 1, :] + base))
        s_n = jnp.where(causal, s_n, NEG_INF)
        o = attend(s_c, s_n, vcache, v_ref[:, hh * DHA:(hh + 1) * DHA])
        oa_ref[:, hh * DHA:(hh + 1) * DHA] = (o * ga_ref[:, hh * DHA:(hh + 1) * DHA].astype(F32)).astype(BF16)

    kv_c = _dot(cckv_ref[...].astype(BF16), wkv_ref[...])
    ckr = ckr_ref[...]
    chunk_ok = (past_len + col) // CHUNK <= (past_len + row) // CHUNK
    for hh in range(HC):
        o_ = hh * HEAD_PAD
        qn = qm_ref[:, o_:o_ + NOPE]
        qr = qm_ref[:, o_ + NOPE:o_ + HEAD_PAD]
        s_c = (lax.dot_general(qn, kv_c[:, o_:o_ + NOPE].astype(BF16), nt, preferred_element_type=F32)
               + lax.dot_general(qr, ckr, nt, preferred_element_type=F32))
        s_n = lax.dot_general(qm_ref[:, o_:o_ + HEAD_PAD], km_ref[:, o_:o_ + HEAD_PAD], nt,
                              preferred_element_type=F32)
        s_n = jnp.where(chunk_ok, s_n, NEG_INF)
        o = attend(s_c, s_n, kv_c[:, o_ + NOPE:o_ + HEAD_PAD].astype(BF16), vc_ref[:, hh * VDIM:(hh + 1) * VDIM])
        oc_ref[:, hh * VDIM:(hh + 1) * VDIM] = (o * gc_ref[:, hh * VDIM:(hh + 1) * VDIM].astype(F32)).astype(BF16)


def _sample_attn(qa, ka, v, ga, cum_col, cum_row, ck, cv, clogf, utri, qm, km, vc, gc, cckv, ckr, wkv,
                 *, batch, ts, past_len):
    row = lambda width: pl.BlockSpec((ts, width), lambda b: (b, 0))
    per_b = lambda shape: pl.BlockSpec((None,) + shape, lambda b: (b, 0, 0))
    return pl.pallas_call(
        functools.partial(_sample_attn_kernel, past_len=past_len),
        grid=(batch,),
        in_specs=[row(4 * HEAD_PAD), row(4 * HEAD_PAD), row(WA), row(WA), row(LANE), per_b((8, ts)),
                  per_b((past_len, WA)), per_b((past_len, WA)), per_b((8, past_len)), _const_spec(utri.shape),
                  row(4 * HEAD_PAD), row(4 * HEAD_PAD), row(WC), row(WC),
                  per_b((past_len, KV_RANK)), per_b((past_len, LANE)), _const_spec(wkv.shape)],
        out_specs=[row(WA), row(WC)],
        out_shape=[jax.ShapeDtypeStruct((batch * ts, WA), BF16), jax.ShapeDtypeStruct((batch * ts, WC), BF16)],
        compiler_params=_params("parallel"),
        name="sample_attn",
    )(qa, ka, v, ga, cum_col, cum_row, ck, cv, clogf, utri, qm, km, vc, gc, cckv, ckr, wkv)


def _merge_kernel(h_ref, oa_ref, ob_ref, oc_ref, od_ref, wga_ref, wgb_ref, wgc_ref, wgd_ref, wb_ref, mg_ref):
    h = h_ref[...]
    merged = None
    for br, (o_ref, wg_ref) in enumerate(zip((oa_ref, ob_ref, oc_ref, od_ref),
                                             (wga_ref, wgb_ref, wgc_ref, wgd_ref))):
        term = jax.nn.sigmoid(_dot(h, wg_ref[...])) * _dot(o_ref[...], wb_ref[br])
        merged = term if merged is None else merged + term
    mg_ref[...] = merged.astype(mg_ref.dtype)


def _merge(h, oa, ob, oc, od, wg, wb, *, tm, tj):
    m, d = h.shape
    nj = d // tj
    row = lambda width: pl.BlockSpec((tm, width), lambda j, i: (i, 0))
    gate = lambda br: pl.BlockSpec((d, tj), lambda j, i: (0, br * nj + j))
    return pl.pallas_call(
        _merge_kernel,
        grid=(nj, m // tm),
        in_specs=[row(d), row(WA), row(WB), row(WC), row(WD), gate(0), gate(1), gate(2), gate(3),
                  pl.BlockSpec((N_BRANCH, WA, tj), lambda j, i: (0, 0, j))],
        out_specs=pl.BlockSpec((tm, tj), lambda j, i: (i, j)),
        out_shape=jax.ShapeDtypeStruct((m, d), BF16),
        compiler_params=_params("parallel", "parallel"),
        name="merge",
    )(h, oa, ob, oc, od, wg, wg, wg, wg, wb)


def _out_proj_kernel(mg_ref, x_ref, wo_ref, gn_ref, *out_refs):
    x = x_ref[...] + _dot(mg_ref[...], wo_ref[...])
    y = x * lax.rsqrt(jnp.mean(x * x, axis=-1, keepdims=True) + EPS)
    hn_ref = out_refs[-1]
    hn_ref[...] = (y * gn_ref[...]).astype(hn_ref.dtype)
    if len(out_refs) == 2:
        out_refs[0][...] = x


def _out_proj(mg, x, wo, gn, *, tm, norm_dtype, want_x):
    m, d = x.shape
    row = pl.BlockSpec((tm, d), lambda i: (i, 0))
    shapes = [jax.ShapeDtypeStruct((m, d), norm_dtype)]
    if want_x:
        shapes = [jax.ShapeDtypeStruct((m, d), F32)] + shapes
    outs = pl.pallas_call(
        _out_proj_kernel,
        grid=(m // tm,),
        in_specs=[row, row, _const_spec(wo.shape), _const_spec((1, d))],
        out_specs=[row] * len(shapes),
        out_shape=shapes,
        compiler_params=_params("parallel"),
        name="out_proj",
    )(mg, x, wo, gn)
    return (outs[0], outs[1]) if want_x else (None, outs[0])


def _rope_tables(pos):
    half = ROPE_DIM // 2
    inv = ROPE_BASE ** (-jnp.arange(half, dtype=F32) / half)
    ang = pos.astype(F32)[:, None] * inv[None, :]
    z = jnp.zeros((pos.shape[0], LANE - ROPE_DIM), F32)
    cos = jnp.concatenate([jnp.cos(ang), jnp.cos(ang), z], axis=-1)
    sin = jnp.concatenate([-jnp.sin(ang), jnp.sin(ang), z], axis=-1)
    return cos, sin


def _pad_cols(a, width):
    return jnp.pad(a, ((0, 0), (0, width - a.shape[1])))


def _layer_weights(l, d, w_in, fox_fb, gm_ln_g, gm_ln_b, mla_q_norm_g, mla_wq_b, mla_kv_norm_g, mla_wkv_b,
                   conv_w, w_branch, w_out):
    sizes = (WA, WA, WA, HA, WA, WB, WB, WB, Q_RANK, KV_RANK, ROPE_DIM, WC, WD, WD, WD, WD, N_BRANCH * d)
    offs = np.concatenate([[0], np.cumsum(sizes)])
    w = w_in[l]
    (a_q, a_k, a_v, a_f, a_g, b_u, b_v, b_g, c_q, c_kv, c_kr, c_g, d_b, d_c, d_h, d_g, m_g) = [
        w[:, int(offs[n]):int(offs[n + 1])] for n in range(len(sizes))]
    half = ROPE_DIM // 2
    swap = lambda a: jnp.concatenate([a[..., half:], a[..., :half]], axis=-1)
    wa = jnp.concatenate([a_q, a_k, a_v, a_g, b_u, b_v, b_g, _pad_cols(a_f, LANE)], axis=1).astype(BF16)
    wb2 = jnp.concatenate([c_q, c_kv, c_g, _pad_cols(c_kr, LANE), _pad_cols(swap(c_kr), LANE),
                           d_b, d_c, d_h, d_g], axis=1).astype(BF16)
    wq = mla_wq_b[l].reshape(Q_RANK, HC, NOPE + ROPE_DIM)
    zpad = jnp.zeros((Q_RANK, HC, LANE - ROPE_DIM), F32)
    wq2 = jnp.concatenate([wq, zpad], axis=-1).reshape(Q_RANK, HC * HEAD_PAD).astype(BF16)
    wqs = jnp.concatenate([swap(wq[..., NOPE:]), zpad], axis=-1).reshape(Q_RANK, HC * LANE).astype(BF16)
    return dict(
        wa=wa, wb2=wb2, wq=wq2, wqs=wqs, wkv=mla_wkv_b[l].astype(BF16),
        fb=_pad_cols(fox_fb[l].reshape(1, HA), LANE),
        lng=gm_ln_g[l].reshape(1, WB), lnb=gm_ln_b[l].reshape(1, WB),
        qg=mla_q_norm_g[l].reshape(1, Q_RANK), kvg=mla_kv_norm_g[l].reshape(1, KV_RANK),
        cw=jnp.pad(conv_w[l], ((0, 8 - CONV_W), (0, 0))),
        wg=m_g.astype(BF16),
        wbr=w_branch[l].astype(BF16), wo=w_out[l].astype(BF16))


def _group_layer(lw, x, h, gn, norm_dtype, want_x, *, batch, seq_len, tm, cos, sin, inj1, inj2, wc, bc, tri, attn):
    tiles_per_batch = max(seq_len // tm, 1)
    fa = _front_a(h, lw["wa"], lw["fb"], lw["lng"], lw["lnb"], wc, bc, tri, tm=tm, tiles_per_batch=tiles_per_batch)
    fb = _front_b(h, lw["wb2"], lw["qg"], lw["kvg"], lw["wq"], lw["wqs"], lw["wkv"], cos, sin, inj1, inj2,
                  lw["cw"], tm=tm, seq_len=seq_len)
    qa, ka, v, ga, k32, v32, logf, cum, ob, vn, vta = fa
    qm, km, vc, gc, ckv32, kr32, od, cin32, vtc = fb
    oa, oc = attn(fa, fb)
    m, d = x.shape
    tm_merge = 2 * tm if m % (2 * tm) == 0 else tm
    mg = _merge(h, oa, ob, oc, od, lw["wg"], lw["wbr"], tm=tm_merge, tj=min(512, d))
    x_new, normed = _out_proj(mg, x, lw["wo"], gn.reshape(1, d), tm=tm, norm_dtype=norm_dtype, want_x=want_x)
    r3 = lambda a: a.reshape(batch, seq_len, a.shape[-1])
    states = dict(k=r3(k32), v=r3(v32), logf=r3(logf)[..., :HA], ckv=r3(ckv32), kr=r3(kr32)[..., :ROPE_DIM],
                  cin=r3(cin32), vn=r3(vn))
    return x_new, normed, states


def kernel(x_prompt, x_sample, cache_fox_k, cache_fox_v, cache_fox_logf, cache_mla_ckv, cache_mla_krope, state_conv, norm_g, w_in, fox_fb, gm_ln_g, gm_ln_b, gm_ws, gm_bs, mla_q_norm_g, mla_wq_b, mla_kv_norm_g, mla_wkv_b, conv_w, w_branch, w_out, final_norm_g):
    bp, tp, d = x_prompt.shape
    bs, ts, _ = x_sample.shape
    depth = w_in.shape[0]
    past = cache_fox_k.shape[2]
    keep = min(tp, past)
    tm_p = min(512, tp)
    tm_s = bs * ts
    t_attn = min(512, tp)
    assert tp % tm_p == 0 and tm_p % GM_CHUNK == 0 and t_attn == tm_p and t_attn % CHUNK == 0
    assert ts <= GM_CHUNK and tm_s % 8 == 0 and past % CHUNK == 0

    xp = x_prompt.reshape(bp * tp, d)
    xs = x_sample.reshape(bs * ts, d)

    cos_p, sin_p = _rope_tables(jnp.arange(tp, dtype=jnp.int32))
    cos_s, sin_s = _rope_tables(past + jnp.arange(ts, dtype=jnp.int32))
    cos_s, sin_s = jnp.tile(cos_s, (bs, 1)), jnp.tile(sin_s, (bs, 1))
    ar_p = np.arange(tm_p)
    tri_p = jnp.asarray(ar_p[None, :] <= ar_p[:, None], BF16)
    ar_s = np.arange(tm_s)
    same_b = (ar_s[None, :] // ts) == (ar_s[:, None] // ts)
    tri_s = jnp.asarray(same_b & (ar_s[None, :] <= ar_s[:, None]), BF16)
    ar_c = np.arange(past)
    utri_c = jnp.asarray(ar_c[:, None] <= ar_c[None, :], BF16)
    zero_inj = jnp.zeros((tm_p, WD), F32)
    tril = np.tril(np.ones((GM_CHUNK, GM_CHUNK), bool))
    bias_fox = _mask_bias(t_attn, 1)
    bias_mla = _mask_bias(t_attn, CHUNK)

    hp = _rmsnorm(xp, norm_g[0], tm_p, BF16)
    hs = _rmsnorm(xs, norm_g[0], tm_s, BF16)
    st_p, st_s = [], []
    for l in range(depth):
        lw = _layer_weights(l, d, w_in, fox_fb, gm_ln_g, gm_ln_b, mla_q_norm_g, mla_wq_b, mla_kv_norm_g,
                            mla_wkv_b, conv_w, w_branch, w_out)
        last = l == depth - 1
        gn = final_norm_g if last else norm_g[l + 1]
        norm_dtype = F32 if last else BF16

        wm = jnp.where(tril, gm_ws[l], 0.0)
        wc_p = wm.astype(BF16)
        bc_p = jnp.repeat(gm_bs[l].T, GC, axis=1)
        wc_s = jnp.einsum("ab,gts->gatbs", jnp.eye(bs, dtype=F32), wm[:, :ts, :ts]).reshape(
            GB, tm_s, tm_s).astype(BF16)
        bc_s = jnp.tile(bc_p[:ts], (bs, 1))

        def attn_p(fa, fb):
            oa = _flash(fa[0], fa[1], fa[10], fa[3], bias_fox, batch=bp, seq_len=tp, heads=HA, t=t_attn,
                        name="flash_fox")
            oc = _flash(fb[0], fb[1], fb[8], fb[3], bias_mla, batch=bp, seq_len=tp, heads=HC, t=t_attn,
                        name="flash_mla")
            return oa, oc

        xp, hp, sp = _group_layer(lw, xp, hp, gn, norm_dtype, not last, batch=bp, seq_len=tp, tm=tm_p,
                                  cos=cos_p, sin=sin_p, inj1=zero_inj, inj2=zero_inj, wc=wc_p, bc=bc_p,
                                  tri=tri_p, attn=attn_p)

        sc = state_conv[l]
        inj1 = jnp.zeros((bs, ts, WD), F32).at[:, 0].set(sc[:, 1]).reshape(tm_s, WD)
        inj2 = jnp.zeros((bs, ts, WD), F32).at[:, 0].set(sc[:, 0]).at[:, 1].set(sc[:, 1]).reshape(tm_s, WD)
        clogf = jnp.pad(jnp.transpose(cache_fox_logf[l], (0, 2, 1)), ((0, 0), (0, 8 - HA), (0, 0)))
        ckr = jnp.pad(cache_mla_krope[l], ((0, 0), (0, 0), (0, LANE - ROPE_DIM))).astype(BF16)

        def attn_s(fa, fb):
            cum_row = jnp.transpose(fa[7].reshape(bs, ts, LANE)[:, :, :8], (0, 2, 1))
            return _sample_attn(fa[0], fa[1], fa[2], fa[3], fa[7], cum_row,
                                cache_fox_k[l].reshape(bs, past, WA), cache_fox_v[l].reshape(bs, past, WA),
                                clogf, utri_c, fb[0], fb[1], fb[2], fb[3], cache_mla_ckv[l], ckr, lw["wkv"],
                                batch=bs, ts=ts, past_len=past)

        xs, hs, ss = _group_layer(lw, xs, hs, gn, norm_dtype, not last, batch=bs, seq_len=ts, tm=tm_s,
                                  cos=cos_s, sin=sin_s, inj1=inj1, inj2=inj2, wc=wc_s, bc=bc_s, tri=tri_s,
                                  attn=attn_s)
        st_p.append(sp)
        st_s.append(ss)

    y_prompt = hp.reshape(bp, tp, d)
    y_sample = hs.reshape(bs, ts, d)
    stk = lambda sts, f: jnp.stack([f(s) for s in sts])
    p_out = (stk(st_p, lambda s: s["k"][:, tp - keep:].reshape(bp, keep, HA, DHA)),
             stk(st_p, lambda s: s["v"][:, tp - keep:].reshape(bp, keep, HA, DHA)),
             stk(st_p, lambda s: s["logf"][:, tp - keep:]),
             stk(st_p, lambda s: s["ckv"][:, tp - keep:]),
             stk(st_p, lambda s: s["kr"][:, tp - keep:]),
             stk(st_p, lambda s: s["cin"][:, tp - (CONV_W - 1):]),
             stk(st_p, lambda s: s["vn"][:, tp - GM_CHUNK:]))
    s_out = (stk(st_s, lambda s: s["k"].reshape(bs, ts, HA, DHA)),
             stk(st_s, lambda s: s["v"].reshape(bs, ts, HA, DHA)),
             stk(st_s, lambda s: s["logf"]),
             stk(st_s, lambda s: s["ckv"]),
             stk(st_s, lambda s: s["kr"]),
             stk(st_s, lambda s: s["cin"][:, ts - (CONV_W - 1):]),
             stk(st_s, lambda s: s["vn"]))
    return (y_prompt, y_sample) + p_out + s_out
```

```python
import functools

import jax
import jax.numpy as jnp
import numpy as np
from jax import lax
from jax.experimental import pallas as pl
from jax.experimental.pallas import tpu as pltpu

F32 = jnp.float32
BF16 = jnp.bfloat16

N_BRANCH = 4
HA, DHA = 4, 128
WA = HA * DHA
WB, GB = 512, 4
GC = WB // GB
GM_CHUNK = 128
HC, NOPE, ROPE_DIM, VDIM = 4, 128, 64, 128
Q_RANK, KV_RANK = 384, 256
WC = HC * VDIM
WD = 512
CONV_W = 3
CHUNK = 64
ROPE_BASE = 10000.0
EPS = 1e-6
NEG_INF = -1e30

LOG2E = 1.4426950408889634
LANE = 128
HEAD_PAD = 2 * LANE
DV_PAD = VDIM + 16
VMEM_LIMIT = 56 * 1024 * 1024

A_Q, A_K, A_V, A_G, B_U, B_V, B_G, A_F = 0, 512, 1024, 1536, 2048, 2560, 3072, 3584
A_COLS = A_F + LANE
C_Q, C_KV, C_G, C_KRA, C_KRB = 0, 384, 640, 1152, 1280
D_B, D_C, D_H, D_G = 1408, 1920, 2432, 2944
B_COLS = D_G + WD


def _dot(a, b):
    return jnp.dot(a, b, preferred_element_type=F32)


def _silu(x):
    return x * jax.nn.sigmoid(x)


def _split3(x):
    hi = x.astype(BF16)
    r = x - hi.astype(F32)
    mid = r.astype(BF16)
    lo = (r - mid.astype(F32)).astype(BF16)
    return hi, mid, lo


def _store_values_transposed(vt_ref, v):
    rows = v.shape[0]
    sub = lax.broadcasted_iota(jnp.int32, (DV_PAD - VDIM, rows), 0)
    ones_row = jnp.where(sub == 0, 1.0, 0.0).astype(BF16)
    for hh in range(v.shape[1] // VDIM):
        vt_ref[0, hh, 0:VDIM, :] = v[:, hh * VDIM:(hh + 1) * VDIM].T.astype(BF16)
        vt_ref[0, hh, VDIM:DV_PAD, :] = ones_row


def _vt_specs(m, tm, heads):
    return (pl.BlockSpec((1, heads, DV_PAD, tm), lambda i: (i, 0, 0, 0)),
            jax.ShapeDtypeStruct((m // tm, heads, DV_PAD, tm), BF16))


def _const_spec(shape):
    nd = len(shape)
    return pl.BlockSpec(shape, lambda *_: (0,) * nd, pipeline_mode=pl.Buffered(1))


def _params(*sem):
    return pltpu.CompilerParams(dimension_semantics=sem, vmem_limit_bytes=VMEM_LIMIT)


def _rmsnorm_kernel(x_ref, g_ref, o_ref):
    x = x_ref[...]
    y = x * lax.rsqrt(jnp.mean(x * x, axis=-1, keepdims=True) + EPS)
    o_ref[...] = (y * g_ref[...]).astype(o_ref.dtype)


def _rmsnorm(x, g, tm, out_dtype):
    m, d = x.shape
    return pl.pallas_call(
        _rmsnorm_kernel,
        grid=(m // tm,),
        in_specs=[pl.BlockSpec((tm, d), lambda i: (i, 0)), _const_spec((1, d))],
        out_specs=pl.BlockSpec((tm, d), lambda i: (i, 0)),
        out_shape=jax.ShapeDtypeStruct((m, d), out_dtype),
        compiler_params=_params("parallel"),
        name="rmsnorm",
    )(x, g.reshape(1, d))


def _front_a_kernel(h_ref, w_ref, fb_ref, lng_ref, lnb_ref, wc_ref, bc_ref, tri_ref,
                    qa_ref, ka_ref, v_ref, ga_ref, k32_ref, v32_ref, logf_ref, cum_ref, ob_ref, vn_ref, vt_ref,
                    carry_ref, *, tiles_per_batch, chunk):
    i = pl.program_id(0)
    tm = h_ref.shape[0]
    h = h_ref[...]

    def seg(off, width=512):
        return _dot(h, w_ref[:, off:off + width])

    x = seg(A_F, LANE) + fb_ref[...]
    logf = jnp.minimum(x, 0.0) - jnp.log1p(jnp.exp(-jnp.abs(x)))
    logf_ref[...] = logf

    @pl.when(i % tiles_per_batch == 0)
    def _():
        carry_ref[...] = jnp.zeros_like(carry_ref)

    tri = tri_ref[...]
    hi, mid, lo = _split3(logf)
    cum = (_dot(tri, hi) + _dot(tri, mid)) + _dot(tri, lo) + carry_ref[0:1, :]
    carry_ref[...] = jnp.broadcast_to(cum[tm - 1:tm, :], carry_ref.shape)
    cum = cum * LOG2E
    cum_ref[...] = cum

    zq = seg(A_Q) * (DHA ** -0.5 * LOG2E)
    zk = seg(A_K)
    zv = seg(A_V)
    k32_ref[...] = zk
    v32_ref[...] = zv
    v_ref[...] = zv.astype(BF16)
    _store_values_transposed(vt_ref, zv)
    ga_ref[...] = _silu(seg(A_G)).astype(BF16)

    lane = lax.broadcasted_iota(jnp.int32, (tm, LANE), 1)
    c_hi = cum.astype(BF16).astype(F32)
    r = cum - c_hi
    c_mid = r.astype(BF16).astype(F32)
    c_lo = r - c_mid
    for hh in range(HA):
        bh = jnp.broadcast_to(c_hi[:, hh:hh + 1], (tm, LANE))
        bm = jnp.broadcast_to(c_mid[:, hh:hh + 1], (tm, LANE))
        bl = jnp.broadcast_to(c_lo[:, hh:hh + 1], (tm, LANE))
        aug_q = jnp.where(lane == 0, bh, jnp.where(lane == 1, bm, jnp.where(lane == 2, bl,
                          jnp.where(lane < 6, 1.0, 0.0))))
        aug_k = jnp.where(lane < 3, 1.0, jnp.where(lane == 3, -bh, jnp.where(lane == 4, -bm,
                          jnp.where(lane == 5, -bl, 0.0))))
        qa_ref[:, hh * HEAD_PAD:hh * HEAD_PAD + DHA] = zq[:, hh * DHA:(hh + 1) * DHA].astype(BF16)
        qa_ref[:, hh * HEAD_PAD + DHA:(hh + 1) * HEAD_PAD] = aug_q.astype(BF16)
        ka_ref[:, hh * HEAD_PAD:hh * HEAD_PAD + DHA] = zk[:, hh * DHA:(hh + 1) * DHA].astype(BF16)
        ka_ref[:, hh * HEAD_PAD + DHA:(hh + 1) * HEAD_PAD] = aug_k.astype(BF16)

    zbv = seg(B_V)
    mu = jnp.mean(zbv, axis=-1, keepdims=True)
    xc = zbv - mu
    var = jnp.mean(xc * xc, axis=-1, keepdims=True)
    vn = xc * lax.rsqrt(var + EPS) * lng_ref[...] + lnb_ref[...]
    vn_ref[...] = vn
    vnb = vn.astype(BF16)
    ug = seg(B_U) * _silu(seg(B_G))
    for ci in range(tm // chunk):
        rows = slice(ci * chunk, (ci + 1) * chunk)
        for g in range(GB):
            cols = slice(g * GC, (g + 1) * GC)
            mix = _dot(wc_ref[g], vnb[rows, cols]) + bc_ref[:, cols]
            ob_ref[rows, cols] = (ug[rows, cols] * mix).astype(BF16)


def _front_a(h, w, fb, lng, lnb, wc, bc, tri, *, tm, tiles_per_batch):
    m, d = h.shape
    chunk = wc.shape[1]
    row = lambda width: pl.BlockSpec((tm, width), lambda i: (i, 0))
    outs = [
        (4 * HEAD_PAD, BF16), (4 * HEAD_PAD, BF16), (WA, BF16), (WA, BF16), (WA, F32), (WA, F32),
        (LANE, F32), (LANE, F32), (WB, BF16), (WB, F32)]
    vt_spec, vt_shape = _vt_specs(m, tm, HA)
    return pl.pallas_call(
        functools.partial(_front_a_kernel, tiles_per_batch=tiles_per_batch, chunk=chunk),
        grid=(m // tm,),
        in_specs=[row(d), _const_spec(w.shape), _const_spec((1, LANE)), _const_spec((1, WB)),
                  _const_spec((1, WB)), _const_spec(wc.shape), _const_spec(bc.shape), _const_spec(tri.shape)],
        out_specs=[row(wd) for wd, _ in outs] + [vt_spec],
        out_shape=[jax.ShapeDtypeStruct((m, wd), dt) for wd, dt in outs] + [vt_shape],
        scratch_shapes=[pltpu.VMEM((8, LANE), F32)],
        compiler_params=_params("arbitrary"),
        name="front_a",
    )(h, w, fb, lng, lnb, wc, bc, tri)


def _front_b_kernel(h_ref, w_ref, qg_ref, kvg_ref, wq_ref, wqs_ref, wkv_ref, cos_ref, sin_ref,
                    inj1_ref, inj2_ref, cw_ref,
                    qm_ref, km_ref, vc_ref, gc_ref, ckv32_ref, kr32_ref, od_ref, cin32_ref, vt_ref,
                    xp_ref, *, seq_len):
    i = pl.program_id(0)
    tm = h_ref.shape[0]
    h = h_ref[...]

    def seg(off, width=512):
        return _dot(h, w_ref[:, off:off + width])

    def rms(x, g):
        return x * lax.rsqrt(jnp.mean(x * x, axis=-1, keepdims=True) + EPS) * g

    cos = cos_ref[...]
    sin = sin_ref[...]
    scale = (NOPE + ROPE_DIM) ** -0.5 * LOG2E

    cq = rms(seg(C_Q, Q_RANK), qg_ref[...]).astype(BF16)
    qf = _dot(cq, wq_ref[...])
    qs = _dot(cq, wqs_ref[...])
    for hh in range(HC):
        o = hh * HEAD_PAD
        qm_ref[:, o:o + NOPE] = (qf[:, o:o + NOPE] * scale).astype(BF16)
        rope = qf[:, o + NOPE:o + HEAD_PAD] * cos + qs[:, hh * LANE:(hh + 1) * LANE] * sin
        qm_ref[:, o + NOPE:o + HEAD_PAD] = (rope * scale).astype(BF16)

    ckv = rms(seg(C_KV, KV_RANK), kvg_ref[...])
    ckv32_ref[...] = ckv
    kv = _dot(ckv.astype(BF16), wkv_ref[...])
    kr = seg(C_KRA, LANE) * cos + seg(C_KRB, LANE) * sin
    kr32_ref[...] = kr
    krb = kr.astype(BF16)
    for hh in range(HC):
        o = hh * HEAD_PAD
        km_ref[:, o:o + NOPE] = kv[:, o:o + NOPE].astype(BF16)
        km_ref[:, o + NOPE:o + HEAD_PAD] = krb
    vcs = jnp.concatenate([kv[:, hh * HEAD_PAD + NOPE:(hh + 1) * HEAD_PAD] for hh in range(HC)], axis=1)
    vc_ref[...] = vcs.astype(BF16)
    _store_values_transposed(vt_ref, vcs)
    gc_ref[...] = _silu(seg(C_G)).astype(BF16)

    @pl.when(i == 0)
    def _():
        xp_ref[0:8, :] = jnp.zeros((8, WD), F32)

    cin = seg(D_C) * seg(D_H)
    cin32_ref[...] = cin
    xp_ref[8:8 + tm, :] = cin
    s1 = xp_ref[7:7 + tm, :]
    s2 = xp_ref[6:6 + tm, :]
    t_local = (i * tm + lax.broadcasted_iota(jnp.int32, (tm, WD), 0)) % seq_len
    s1 = jnp.where(t_local == 0, inj1_ref[...], s1)
    s2 = jnp.where(t_local < 2, inj2_ref[...], s2)
    conv = s2 * cw_ref[0:1, :] + s1 * cw_ref[1:2, :] + cin * cw_ref[2:3, :]
    od_ref[...] = (seg(D_B) * conv * _silu(seg(D_G))).astype(BF16)
    xp_ref[0:8, :] = xp_ref[tm:tm + 8, :]


def _front_b(h, w, qg, kvg, wq, wqs, wkv, cos, sin, inj1, inj2, cw, *, tm, seq_len):
    m, d = h.shape
    row = lambda width: pl.BlockSpec((tm, width), lambda i: (i, 0))
    n_pos = cos.shape[0] // tm
    n_inj = inj1.shape[0] // tm
    pos_spec = pl.BlockSpec((tm, LANE), lambda i: (i % n_pos, 0))
    inj_spec = pl.BlockSpec((tm, WD), lambda i: (i % n_inj, 0))
    outs = [(4 * HEAD_PAD, BF16), (4 * HEAD_PAD, BF16), (WC, BF16), (WC, BF16), (KV_RANK, F32),
            (LANE, F32), (WD, BF16), (WD, F32)]
    vt_spec, vt_shape = _vt_specs(m, tm, HC)
    return pl.pallas_call(
        functools.partial(_front_b_kernel, seq_len=seq_len),
        grid=(m // tm,),
        in_specs=[row(d), _const_spec(w.shape), _const_spec((1, Q_RANK)), _const_spec((1, KV_RANK)),
                  _const_spec(wq.shape), _const_spec(wqs.shape), _const_spec(wkv.shape),
                  pos_spec, pos_spec, inj_spec, inj_spec, _const_spec(cw.shape)],
        out_specs=[row(wd) for wd, _ in outs] + [vt_spec],
        out_shape=[jax.ShapeDtypeStruct((m, wd), dt) for wd, dt in outs] + [vt_shape],
        scratch_shapes=[pltpu.VMEM((tm + 8, WD), F32)],
        compiler_params=_params("arbitrary"),
        name="front_b",
    )(h, w, qg, kvg, wq, wqs, wkv, cos, sin, inj1, inj2, cw)


FLASH_HEADS = 2


def _flash_kernel(q_ref, k_ref, vt_ref, g_ref, bias_ref, o_ref,
                  s_ref, p_ref, pm_ref, a_ref, m_ref, acc_ref, *, tq, tk, hp):
    qi = pl.program_id(2)
    nk = k_ref.shape[0] // tk
    r = tq // tk
    nt = (((1,), (1,)), ((), ()))

    def qk(j, slot):
        jc = jnp.minimum(j, nk - 1)
        bias = jnp.concatenate([bias_ref[jnp.clip(j - (r * qi + u) + 1, 0, 2)] for u in range(r)], axis=1)
        for h in range(hp):
            kb = k_ref[pl.ds(pl.multiple_of(jc * tk, tk), tk), h * HEAD_PAD:(h + 1) * HEAD_PAD]
            q = q_ref[:, h * HEAD_PAD:(h + 1) * HEAD_PAD]
            s = lax.dot_general(kb, q, nt, preferred_element_type=F32) + bias
            s_ref[slot, h] = s
            pm = s[0:8, :]
            for c in range(1, tk // 8):
                pm = jnp.maximum(pm, s[c * 8:(c + 1) * 8, :])
            pm_ref[slot, h] = pm

    def softmax(slot):
        for h in range(hp):
            m_old = m_ref[h]
            m_new = jnp.maximum(m_old, jnp.max(pm_ref[slot, h], axis=0, keepdims=True))
            a_ref[slot, h] = jnp.exp2(m_old - m_new)
            m_ref[h] = m_new
            p_ref[slot, h] = jnp.exp2(s_ref[slot, h] - m_new).astype(BF16)

    def pv(j, slot):
        jc = jnp.clip(j, 0, nk - 1)
        for h in range(hp):
            acc_ref[h] = a_ref[slot, h] * acc_ref[h] + _dot(vt_ref[jc, h], p_ref[slot, h])

    m_ref[...] = jnp.full_like(m_ref, NEG_INF)
    acc_ref[...] = jnp.zeros_like(acc_ref)
    p_ref[1] = jnp.zeros_like(p_ref[1])
    a_ref[1] = jnp.ones_like(a_ref[1])
    qk(0, 0)

    def pair(jj, c):
        j = 2 * jj
        qk(j + 1, 1)
        softmax(0)
        pv(j - 1, 1)
        qk(j + 2, 0)
        softmax(1)
        pv(j, 0)
        return c

    n_pairs = (r * (qi + 1) + 1) // 2
    lax.fori_loop(0, n_pairs, pair, 0)
    pv(2 * n_pairs - 1, 1)
    for h in range(hp):
        acc = acc_ref[h]
        o = (acc[:VDIM, :] / acc[VDIM:VDIM + 1, :]).T
        cols = slice(h * VDIM, (h + 1) * VDIM)
        o_ref[:, cols] = (o * g_ref[:, cols].astype(F32)).astype(o_ref.dtype)


def _flash(q, k, vt, g, bias, *, batch, seq_len, heads, tq, name):
    tk = vt.shape[3]
    nq, nk = seq_len // tq, seq_len // tk
    hp = FLASH_HEADS
    q_spec = pl.BlockSpec((tq, hp * HEAD_PAD), lambda b, h, i: (b * nq + i, h))
    o_spec = pl.BlockSpec((tq, hp * VDIM), lambda b, h, i: (b * nq + i, h))
    once = pl.Buffered(1)
    return pl.pallas_call(
        functools.partial(_flash_kernel, tq=tq, tk=tk, hp=hp),
        grid=(batch, heads // hp, nq),
        in_specs=[q_spec,
                  pl.BlockSpec((seq_len, hp * HEAD_PAD), lambda b, h, i: (b, h), pipeline_mode=once),
                  pl.BlockSpec((nk, hp, DV_PAD, tk), lambda b, h, i: (b, h, 0, 0), pipeline_mode=once),
                  o_spec,
                  _const_spec(bias.shape)],
        out_specs=o_spec,
        out_shape=jax.ShapeDtypeStruct(g.shape, BF16),
        scratch_shapes=[pltpu.VMEM((2, hp, tk, tq), F32), pltpu.VMEM((2, hp, tk, tq), BF16),
                        pltpu.VMEM((2, hp, 8, tq), F32), pltpu.VMEM((2, hp, 1, tq), F32),
                        pltpu.VMEM((hp, 1, tq), F32), pltpu.VMEM((hp, DV_PAD, tq), F32)],
        compiler_params=_params("parallel", "parallel", "arbitrary"),
        name=name,
    )(q, k, vt, g, bias)


def _mask_bias(t, gran):
    key = np.arange(t)[:, None] // gran
    qry = np.arange(t)[None, :] // gran
    diag = np.where(key <= qry, 0.0, NEG_INF)
    return jnp.asarray(np.stack([np.zeros((t, t)), diag, np.full((t, t), NEG_INF)]), F32)


def _sample_attn_kernel(qa_ref, ka_ref, v_ref, ga_ref, cumc_ref, cumr_ref, ck_ref, cv_ref, clogf_ref, utri_ref,
                        qm_ref, km_ref, vc_ref, gc_ref, cckv_ref, ckr_ref, wkv_ref,
                        oa_ref, oc_ref, *, past_len):
    ts = qa_ref.shape[0]
    row = lax.broadcasted_iota(jnp.int32, (ts, ts), 0)
    col = lax.broadcasted_iota(jnp.int32, (ts, ts), 1)
    nt = (((1,), (1,)), ((), ()))

    def attend(s_c, s_n, v_c, v_n):
        m = jnp.maximum(jnp.max(s_c, axis=-1, keepdims=True), jnp.max(s_n, axis=-1, keepdims=True))
        p_c = jnp.exp2(s_c - m)
        p_n = jnp.exp2(s_n - m)
        l = jnp.sum(p_c, axis=-1, keepdims=True) + jnp.sum(p_n, axis=-1, keepdims=True)
        return (_dot(p_c.astype(BF16), v_c) + _dot(p_n.astype(BF16), v_n)) / l

    utri = utri_ref[...]
    hi, mid, lo = _split3(clogf_ref[...])
    c_cache = ((_dot(hi, utri) + _dot(mid, utri)) + _dot(lo, utri)) * LOG2E
    cum_col = cumc_ref[...]
    cum_row = cumr_ref[...]
    causal = col <= row
    for hh in range(HA):
        q = qa_ref[:, hh * HEAD_PAD:hh * HEAD_PAD + DHA]
        kn = ka_ref[:, hh * HEAD_PAD:hh * HEAD_PAD + DHA]
        kc = ck_ref[:, hh * DHA:(hh + 1) * DHA].astype(BF16)
        vcache = cv_ref[:, hh * DHA:(hh + 1) * DHA].astype(BF16)
        base = c_cache[hh:hh + 1, past_len - 1:past_len]
        cq = cum_col[:, hh:hh + 1] + base
        s_c = lax.dot_general(q, kc, nt, preferred_element_type=F32) + (cq - c_cache[hh:hh + 1, :])
        s_n = lax.dot_general(q, kn, nt, preferred_element_type=F32) + (cq - (cum_row[hh:hh + 1, :] + base))
        s_n = jnp.where(causal, s_n, NEG_INF)
        o = attend(s_c, s_n, vcache, v_ref[:, hh * DHA:(hh + 1) * DHA])
        oa_ref[:, hh * DHA:(hh + 1) * DHA] = (o * ga_ref[:, hh * DHA:(hh + 1) * DHA].astype(F32)).astype(BF16)

    kv_c = _dot(cckv_ref[...].astype(BF16), wkv_ref[...])
    ckr = ckr_ref[...]
    chunk_ok = (past_len + col) // CHUNK <= (past_len + row) // CHUNK
    for hh in range(HC):
        o_ = hh * HEAD_PAD
        qn = qm_ref[:, o_:o_ + NOPE]
        qr = qm_ref[:, o_ + NOPE:o_ + HEAD_PAD]
        s_c = (lax.dot_general(qn, kv_c[:, o_:o_ + NOPE].astype(BF16), nt, preferred_element_type=F32)
               + lax.dot_general(qr, ckr, nt, preferred_element_type=F32))
        s_n = lax.dot_general(qm_ref[:, o_:o_ + HEAD_PAD], km_ref[:, o_:o_ + HEAD_PAD], nt,
                              preferred_element_type=F32)
        s_n = jnp.where(chunk_ok, s_n, NEG_INF)
        o = attend(s_c, s_n, kv_c[:, o_ + NOPE:o_ + HEAD_PAD].astype(BF16), vc_ref[:, hh * VDIM:(hh + 1) * VDIM])
        oc_ref[:, hh * VDIM:(hh + 1) * VDIM] = (o * gc_ref[:, hh * VDIM:(hh + 1) * VDIM].astype(F32)).astype(BF16)


def _sample_attn(qa, ka, v, ga, cum_col, cum_row, ck, cv, clogf, utri, qm, km, vc, gc, cckv, ckr, wkv,
                 *, batch, ts, past_len, layer):
    row = lambda width: pl.BlockSpec((ts, width), lambda b: (b, 0))
    per_b = lambda shape: pl.BlockSpec((None,) + shape, lambda b: (b, 0, 0))
    cache = lambda width: pl.BlockSpec((None, past_len, width), lambda b: (layer * batch + b, 0, 0))
    return pl.pallas_call(
        functools.partial(_sample_attn_kernel, past_len=past_len),
        grid=(batch,),
        in_specs=[row(4 * HEAD_PAD), row(4 * HEAD_PAD), row(WA), row(WA), row(LANE), per_b((8, ts)),
                  cache(WA), cache(WA), per_b((8, past_len)), _const_spec(utri.shape),
                  row(4 * HEAD_PAD), row(4 * HEAD_PAD), row(WC), row(WC),
                  cache(KV_RANK), per_b((past_len, LANE)), _const_spec(wkv.shape)],
        out_specs=[row(WA), row(WC)],
        out_shape=[jax.ShapeDtypeStruct((batch * ts, WA), BF16), jax.ShapeDtypeStruct((batch * ts, WC), BF16)],
        compiler_params=_params("parallel"),
        name="sample_attn",
    )(qa, ka, v, ga, cum_col, cum_row, ck, cv, clogf, utri, qm, km, vc, gc, cckv, ckr, wkv)


def _merge_kernel(h_ref, oa_ref, ob_ref, oc_ref, od_ref, wga_ref, wgb_ref, wgc_ref, wgd_ref, wb_ref, mg_ref):
    h = h_ref[...]
    merged = None
    for br, (o_ref, wg_ref) in enumerate(zip((oa_ref, ob_ref, oc_ref, od_ref),
                                             (wga_ref, wgb_ref, wgc_ref, wgd_ref))):
        term = jax.nn.sigmoid(_dot(h, wg_ref[...])) * _dot(o_ref[...], wb_ref[br])
        merged = term if merged is None else merged + term
    mg_ref[...] = merged.astype(mg_ref.dtype)


def _merge(h, oa, ob, oc, od, wg, wb, *, tm, tj):
    m, d = h.shape
    nj = d // tj
    row = lambda width: pl.BlockSpec((tm, width), lambda j, i: (i, 0))
    gate = lambda br: pl.BlockSpec((d, tj), lambda j, i: (0, br * nj + j))
    return pl.pallas_call(
        _merge_kernel,
        grid=(nj, m // tm),
        in_specs=[row(d), row(WA), row(WB), row(WC), row(WD), gate(0), gate(1), gate(2), gate(3),
                  pl.BlockSpec((N_BRANCH, WA, tj), lambda j, i: (0, 0, j))],
        out_specs=pl.BlockSpec((tm, tj), lambda j, i: (i, j)),
        out_shape=jax.ShapeDtypeStruct((m, d), BF16),
        compiler_params=_params("parallel", "parallel"),
        name="merge",
    )(h, oa, ob, oc, od, wg, wg, wg, wg, wb)


def _out_proj_kernel(mg_ref, x_ref, wo_ref, gn_ref, *out_refs):
    x = x_ref[...] + _dot(mg_ref[...], wo_ref[...])
    y = x * lax.rsqrt(jnp.mean(x * x, axis=-1, keepdims=True) + EPS)
    hn_ref = out_refs[-1]
    hn_ref[...] = (y * gn_ref[...]).astype(hn_ref.dtype)
    if len(out_refs) == 2:
        out_refs[0][...] = x


def _out_proj(mg, x, wo, gn, *, tm, norm_dtype, want_x):
    m, d = x.shape
    row = pl.BlockSpec((tm, d), lambda i: (i, 0))
    shapes = [jax.ShapeDtypeStruct((m, d), norm_dtype)]
    if want_x:
        shapes = [jax.ShapeDtypeStruct((m, d), F32)] + shapes
    outs = pl.pallas_call(
        _out_proj_kernel,
        grid=(m // tm,),
        in_specs=[row, row, _const_spec(wo.shape), _const_spec((1, d))],
        out_specs=[row] * len(shapes),
        out_shape=shapes,
        compiler_params=_params("parallel"),
        name="out_proj",
    )(mg, x, wo, gn)
    return (outs[0], outs[1]) if want_x else (None, outs[0])


def _rope_tables(pos):
    half = ROPE_DIM // 2
    inv = ROPE_BASE ** (-jnp.arange(half, dtype=F32) / half)
    ang = pos.astype(F32)[:, None] * inv[None, :]
    z = jnp.zeros((pos.shape[0], LANE - ROPE_DIM), F32)
    cos = jnp.concatenate([jnp.cos(ang), jnp.cos(ang), z], axis=-1)
    sin = jnp.concatenate([-jnp.sin(ang), jnp.sin(ang), z], axis=-1)
    return cos, sin


def _pad_cols(a, width):
    return jnp.pad(a, ((0, 0), (0, width - a.shape[1])))


def _repack_kernel(w_ref, wa_ref, wb_ref, wg_ref, *, offs):
    (a_q, _, _, a_f, a_g, _, _, _, c_q, _, c_kr, c_g, d_b, _, _, _, m_g, end) = offs
    x = w_ref[...]
    lane = lax.broadcasted_iota(jnp.int32, (x.shape[0], LANE), 1)
    half = ROPE_DIM // 2
    f_blk = jnp.where(lane < HA, x[:, a_f:a_f + LANE], 0.0)
    kr_a = jnp.where(lane < ROPE_DIM, x[:, c_kr:c_kr + LANE], 0.0)
    kr_b = jnp.where(lane < half, x[:, c_kr + half:c_kr + half + LANE],
                     jnp.where(lane < ROPE_DIM, x[:, c_kr - half:c_kr - half + LANE], 0.0))
    wa_ref[...] = jnp.concatenate([x[:, a_q:a_f], x[:, a_g:c_q], f_blk], axis=1).astype(BF16)
    wb_ref[...] = jnp.concatenate([x[:, c_q:c_kr], x[:, c_g:d_b], kr_a, kr_b, x[:, d_b:m_g]], axis=1).astype(BF16)
    wg_ref[...] = x[:, m_g:end].astype(BF16)


def _repack(w_in, l):
    _, d, n_in = w_in.shape
    sizes = (WA, WA, WA, HA, WA, WB, WB, WB, Q_RANK, KV_RANK, ROPE_DIM, WC, WD, WD, WD, WD, N_BRANCH * d)
    offs = tuple(int(o) for o in np.concatenate([[0], np.cumsum(sizes)]))
    assert offs[-1] == n_in
    tr = min(128, d)
    widths = (A_COLS, B_COLS, N_BRANCH * d)
    return pl.pallas_call(
        functools.partial(_repack_kernel, offs=offs),
        grid=(d // tr,),
        in_specs=[pl.BlockSpec((None, tr, n_in), lambda i: (l, i, 0))],
        out_specs=[pl.BlockSpec((tr, wd), lambda i: (i, 0)) for wd in widths],
        out_shape=[jax.ShapeDtypeStruct((d, wd), BF16) for wd in widths],
        compiler_params=_params("parallel"),
        name="repack",
    )(w_in)


def _layer_weights(l, d, w_in, fox_fb, gm_ln_g, gm_ln_b, mla_q_norm_g, mla_wq_b, mla_kv_norm_g, mla_wkv_b,
                   conv_w, w_branch, w_out):
    wa, wb2, wg = _repack(w_in, l)
    half = ROPE_DIM // 2
    swap = lambda a: jnp.concatenate([a[..., half:], a[..., :half]], axis=-1)
    wq = mla_wq_b[l].reshape(Q_RANK, HC, NOPE + ROPE_DIM)
    zpad = jnp.zeros((Q_RANK, HC, LANE - ROPE_DIM), F32)
    wq2 = jnp.concatenate([wq, zpad], axis=-1).reshape(Q_RANK, HC * HEAD_PAD).astype(BF16)
    wqs = jnp.concatenate([swap(wq[..., NOPE:]), zpad], axis=-1).reshape(Q_RANK, HC * LANE).astype(BF16)
    return dict(
        wa=wa, wb2=wb2, wq=wq2, wqs=wqs, wkv=mla_wkv_b[l].astype(BF16),
        fb=_pad_cols(fox_fb[l].reshape(1, HA), LANE),
        lng=gm_ln_g[l].reshape(1, WB), lnb=gm_ln_b[l].reshape(1, WB),
        qg=mla_q_norm_g[l].reshape(1, Q_RANK), kvg=mla_kv_norm_g[l].reshape(1, KV_RANK),
        cw=jnp.pad(conv_w[l], ((0, 8 - CONV_W), (0, 0))),
        wg=wg,
        wbr=w_branch[l].astype(BF16), wo=w_out[l].astype(BF16))


def _group_layer(lw, x, h, gn, norm_dtype, want_x, *, batch, seq_len, tm, cos, sin, inj1, inj2, wc, bc, tri, attn):
    tiles_per_batch = max(seq_len // tm, 1)
    fa = _front_a(h, lw["wa"], lw["fb"], lw["lng"], lw["lnb"], wc, bc, tri, tm=tm, tiles_per_batch=tiles_per_batch)
    fb = _front_b(h, lw["wb2"], lw["qg"], lw["kvg"], lw["wq"], lw["wqs"], lw["wkv"], cos, sin, inj1, inj2,
                  lw["cw"], tm=tm, seq_len=seq_len)
    qa, ka, v, ga, k32, v32, logf, cum, ob, vn, vta = fa
    qm, km, vc, gc, ckv32, kr32, od, cin32, vtc = fb
    oa, oc = attn(fa, fb)
    m, d = x.shape
    tm_merge = 2 * tm if m % (2 * tm) == 0 else tm
    mg = _merge(h, oa, ob, oc, od, lw["wg"], lw["wbr"], tm=tm_merge, tj=min(512, d))
    x_new, normed = _out_proj(mg, x, lw["wo"], gn.reshape(1, d), tm=tm, norm_dtype=norm_dtype, want_x=want_x)
    r3 = lambda a: a.reshape(batch, seq_len, a.shape[-1])
    states = dict(k=r3(k32), v=r3(v32), logf=r3(logf)[..., :HA], ckv=r3(ckv32), kr=r3(kr32)[..., :ROPE_DIM],
                  cin=r3(cin32), vn=r3(vn))
    return x_new, normed, states


def kernel(x_prompt, x_sample, cache_fox_k, cache_fox_v, cache_fox_logf, cache_mla_ckv, cache_mla_krope, state_conv, norm_g, w_in, fox_fb, gm_ln_g, gm_ln_b, gm_ws, gm_bs, mla_q_norm_g, mla_wq_b, mla_kv_norm_g, mla_wkv_b, conv_w, w_branch, w_out, final_norm_g):
    bp, tp, d = x_prompt.shape
    bs, ts, _ = x_sample.shape
    depth = w_in.shape[0]
    past = cache_fox_k.shape[2]
    keep = min(tp, past)
    tm_p = min(512, tp)
    tm_s = bs * ts
    t_attn = tm_p
    tq_attn = 2 * t_attn if tp % (2 * t_attn) == 0 else t_attn
    assert tp % tm_p == 0 and tm_p % GM_CHUNK == 0 and t_attn % CHUNK == 0
    assert ts <= GM_CHUNK and tm_s % 8 == 0 and past % CHUNK == 0

    xp = x_prompt.reshape(bp * tp, d)
    xs = x_sample.reshape(bs * ts, d)

    cos_p, sin_p = _rope_tables(jnp.arange(tp, dtype=jnp.int32))
    cos_s, sin_s = _rope_tables(past + jnp.arange(ts, dtype=jnp.int32))
    cos_s, sin_s = jnp.tile(cos_s, (bs, 1)), jnp.tile(sin_s, (bs, 1))
    ar_p = np.arange(tm_p)
    tri_p = jnp.asarray(ar_p[None, :] <= ar_p[:, None], BF16)
    ar_s = np.arange(tm_s)
    same_b = (ar_s[None, :] // ts) == (ar_s[:, None] // ts)
    tri_s = jnp.asarray(same_b & (ar_s[None, :] <= ar_s[:, None]), BF16)
    ar_c = np.arange(past)
    utri_c = jnp.asarray(ar_c[:, None] <= ar_c[None, :], BF16)
    zero_inj = jnp.zeros((tm_p, WD), F32)
    tril = np.tril(np.ones((GM_CHUNK, GM_CHUNK), bool))
    bias_fox = _mask_bias(t_attn, 1)
    bias_mla = _mask_bias(t_attn, CHUNK)

    hp = _rmsnorm(xp, norm_g[0], tm_p, BF16)
    hs = _rmsnorm(xs, norm_g[0], tm_s, BF16)
    st_p, st_s = [], []
    for l in range(depth):
        lw = _layer_weights(l, d, w_in, fox_fb, gm_ln_g, gm_ln_b, mla_q_norm_g, mla_wq_b, mla_kv_norm_g,
                            mla_wkv_b, conv_w, w_branch, w_out)
        last = l == depth - 1
        gn = final_norm_g if last else norm_g[l + 1]
        norm_dtype = F32 if last else BF16

        wm = jnp.where(tril, gm_ws[l], 0.0)
        wc_p = wm.astype(BF16)
        bc_p = jnp.repeat(gm_bs[l].T, GC, axis=1)
        wc_s = jnp.einsum("ab,gts->gatbs", jnp.eye(bs, dtype=F32), wm[:, :ts, :ts]).reshape(
            GB, tm_s, tm_s).astype(BF16)
        bc_s = jnp.tile(bc_p[:ts], (bs, 1))

        def attn_p(fa, fb):
            oa = _flash(fa[0], fa[1], fa[10], fa[3], bias_fox, batch=bp, seq_len=tp, heads=HA, tq=tq_attn,
                        name="flash_fox")
            oc = _flash(fb[0], fb[1], fb[8], fb[3], bias_mla, batch=bp, seq_len=tp, heads=HC, tq=tq_attn,
                        name="flash_mla")
            return oa, oc

        xp, hp, sp = _group_layer(lw, xp, hp, gn, norm_dtype, not last, batch=bp, seq_len=tp, tm=tm_p,
                                  cos=cos_p, sin=sin_p, inj1=zero_inj, inj2=zero_inj, wc=wc_p, bc=bc_p,
                                  tri=tri_p, attn=attn_p)

        sc = state_conv[l]
        inj1 = jnp.zeros((bs, ts, WD), F32).at[:, 0].set(sc[:, 1]).reshape(tm_s, WD)
        inj2 = jnp.zeros((bs, ts, WD), F32).at[:, 0].set(sc[:, 0]).at[:, 1].set(sc[:, 1]).reshape(tm_s, WD)
        clogf = jnp.pad(jnp.transpose(cache_fox_logf[l], (0, 2, 1)), ((0, 0), (0, 8 - HA), (0, 0)))
        ckr = jnp.pad(cache_mla_krope[l], ((0, 0), (0, 0), (0, LANE - ROPE_DIM))).astype(BF16)

        def attn_s(fa, fb):
            cum_row = jnp.transpose(fa[7].reshape(bs, ts, LANE)[:, :, :8], (0, 2, 1))
            return _sample_attn(fa[0], fa[1], fa[2], fa[3], fa[7], cum_row,
                                cache_fox_k.reshape(depth * bs, past, WA), cache_fox_v.reshape(depth * bs, past, WA),
                                clogf, utri_c, fb[0], fb[1], fb[2], fb[3],
                                cache_mla_ckv.reshape(depth * bs, past, KV_RANK), ckr, lw["wkv"],
                                batch=bs, ts=ts, past_len=past, layer=l)

        xs, hs, ss = _group_layer(lw, xs, hs, gn, norm_dtype, not last, batch=bs, seq_len=ts, tm=tm_s,
                                  cos=cos_s, sin=sin_s, inj1=inj1, inj2=inj2, wc=wc_s, bc=bc_s, tri=tri_s,
                                  attn=attn_s)
        st_p.append(sp)
        st_s.append(ss)

    y_prompt = hp.reshape(bp, tp, d)
    y_sample = hs.reshape(bs, ts, d)
    stk = lambda sts, f: jnp.stack([f(s) for s in sts])
    p_out = (stk(st_p, lambda s: s["k"][:, tp - keep:].reshape(bp, keep, HA, DHA)),
             stk(st_p, lambda s: s["v"][:, tp - keep:].reshape(bp, keep, HA, DHA)),
             stk(st_p, lambda s: s["logf"][:, tp - keep:]),
             stk(st_p, lambda s: s["ckv"][:, tp - keep:]),
             stk(st_p, lambda s: s["kr"][:, tp - keep:]),
             stk(st_p, lambda s: s["cin"][:, tp - (CONV_W - 1):]),
             stk(st_p, lambda s: s["vn"][:, tp - GM_CHUNK:]))
    s_out = (stk(st_s, lambda s: s["k"].reshape(bs, ts, HA, DHA)),
             stk(st_s, lambda s: s["v"].reshape(bs, ts, HA, DHA)),
             stk(st_s, lambda s: s["logf"]),
             stk(st_s, lambda s: s["ckv"]),
             stk(st_s, lambda s: s["kr"]),
             stk(st_s, lambda s: s["cin"][:, ts - (CONV_W - 1):]),
             stk(st_s, lambda s: s["vn"]))
    return (y_prompt, y_sample) + p_out + s_out
```

```python
import functools

import jax
import jax.numpy as jnp
import numpy as np
from jax import lax
from jax.experimental import pallas as pl
from jax.experimental.pallas import tpu as pltpu

F32 = jnp.float32
BF16 = jnp.bfloat16

N_BRANCH = 4
HA, DHA = 4, 128
WA = HA * DHA
WB, GB = 512, 4
GC = WB // GB
GM_CHUNK = 128
HC, NOPE, ROPE_DIM, VDIM = 4, 128, 64, 128
Q_RANK, KV_RANK = 384, 256
WC = HC * VDIM
WD = 512
CONV_W = 3
CHUNK = 64
ROPE_BASE = 10000.0
EPS = 1e-6
NEG_INF = -1e30

LOG2E = 1.4426950408889634
LANE = 128
HEAD_PAD = 2 * LANE
DV_PAD = VDIM + 16
VMEM_LIMIT = 56 * 1024 * 1024

A_Q, A_K, A_V, A_G, B_U, B_V, B_G, A_F = 0, 512, 1024, 1536, 2048, 2560, 3072, 3584
A_COLS = A_F + LANE
C_Q, C_KV, C_G, C_KR = 0, 384, 640, 1152
D_B, D_C, D_H, D_G = 1280, 1792, 2304, 2816
B_COLS = D_G + WD


def _dot(a, b):
    return jnp.dot(a, b, preferred_element_type=F32)


def _silu(x):
    return x * jax.nn.sigmoid(x)


def _split3(x):
    hi = x.astype(BF16)
    r = x - hi.astype(F32)
    mid = r.astype(BF16)
    lo = (r - mid.astype(F32)).astype(BF16)
    return hi, mid, lo


def _store_values_transposed(vt_ref, v):
    rows = v.shape[0]
    sub = lax.broadcasted_iota(jnp.int32, (DV_PAD - VDIM, rows), 0)
    ones_row = jnp.where(sub == 0, 1.0, 0.0).astype(BF16)
    for hh in range(v.shape[1] // VDIM):
        vt_ref[0, hh, 0:VDIM, :] = v[:, hh * VDIM:(hh + 1) * VDIM].T.astype(BF16)
        vt_ref[0, hh, VDIM:DV_PAD, :] = ones_row


def _vt_specs(m, tm, heads):
    return (pl.BlockSpec((1, heads, DV_PAD, tm), lambda i: (i, 0, 0, 0)),
            jax.ShapeDtypeStruct((m // tm, heads, DV_PAD, tm), BF16))


def _layer_spec(w, layer):
    return pl.BlockSpec((None,) + w.shape[1:], lambda *_: (layer, 0, 0), pipeline_mode=pl.Buffered(1))


def _const_spec(shape):
    nd = len(shape)
    return pl.BlockSpec(shape, lambda *_: (0,) * nd, pipeline_mode=pl.Buffered(1))


def _params(*sem):
    return pltpu.CompilerParams(dimension_semantics=sem, vmem_limit_bytes=VMEM_LIMIT)


def _rmsnorm_kernel(x_ref, g_ref, o_ref):
    x = x_ref[...]
    y = x * lax.rsqrt(jnp.mean(x * x, axis=-1, keepdims=True) + EPS)
    o_ref[...] = (y * g_ref[...]).astype(o_ref.dtype)


def _rmsnorm(x, g, tm, out_dtype):
    m, d = x.shape
    return pl.pallas_call(
        _rmsnorm_kernel,
        grid=(m // tm,),
        in_specs=[pl.BlockSpec((tm, d), lambda i: (i, 0)), _const_spec((1, d))],
        out_specs=pl.BlockSpec((tm, d), lambda i: (i, 0)),
        out_shape=jax.ShapeDtypeStruct((m, d), out_dtype),
        compiler_params=_params("parallel"),
        name="rmsnorm",
    )(x, g.reshape(1, d))


def _front_a_kernel(h_ref, w_ref, fb_ref, lng_ref, lnb_ref, wc_ref, bc_ref, tri_ref,
                    qa_ref, ka_ref, v_ref, ga_ref, k32_ref, v32_ref, logf_ref, cum_ref, ob_ref, vn_ref, vt_ref,
                    carry_ref, *, tiles_per_batch, chunk):
    i = pl.program_id(0)
    tm = h_ref.shape[0]
    h = h_ref[...]

    def seg(off, width=512):
        return _dot(h, w_ref[:, off:off + width])

    x = seg(A_F, LANE) + fb_ref[...]
    logf = jnp.minimum(x, 0.0) - jnp.log1p(jnp.exp(-jnp.abs(x)))
    logf_ref[...] = logf

    @pl.when(i % tiles_per_batch == 0)
    def _():
        carry_ref[...] = jnp.zeros_like(carry_ref)

    tri = tri_ref[...]
    hi, mid, lo = _split3(logf)
    cum = (_dot(tri, hi) + _dot(tri, mid)) + _dot(tri, lo) + carry_ref[0:1, :]
    carry_ref[...] = jnp.broadcast_to(cum[tm - 1:tm, :], carry_ref.shape)
    cum = cum * LOG2E
    cum_ref[...] = cum

    zq = seg(A_Q) * (DHA ** -0.5 * LOG2E)
    zk = seg(A_K)
    zv = seg(A_V)
    k32_ref[...] = zk
    v32_ref[...] = zv
    v_ref[...] = zv.astype(BF16)
    _store_values_transposed(vt_ref, zv)
    ga_ref[...] = _silu(seg(A_G)).astype(BF16)

    lane = lax.broadcasted_iota(jnp.int32, (tm, LANE), 1)
    c_hi = cum.astype(BF16).astype(F32)
    r = cum - c_hi
    c_mid = r.astype(BF16).astype(F32)
    c_lo = r - c_mid
    for hh in range(HA):
        bh = jnp.broadcast_to(c_hi[:, hh:hh + 1], (tm, LANE))
        bm = jnp.broadcast_to(c_mid[:, hh:hh + 1], (tm, LANE))
        bl = jnp.broadcast_to(c_lo[:, hh:hh + 1], (tm, LANE))
        aug_q = jnp.where(lane == 0, bh, jnp.where(lane == 1, bm, jnp.where(lane == 2, bl,
                          jnp.where(lane < 6, 1.0, 0.0))))
        aug_k = jnp.where(lane < 3, 1.0, jnp.where(lane == 3, -bh, jnp.where(lane == 4, -bm,
                          jnp.where(lane == 5, -bl, 0.0))))
        qa_ref[:, hh * HEAD_PAD:hh * HEAD_PAD + DHA] = zq[:, hh * DHA:(hh + 1) * DHA].astype(BF16)
        qa_ref[:, hh * HEAD_PAD + DHA:(hh + 1) * HEAD_PAD] = aug_q.astype(BF16)
        ka_ref[:, hh * HEAD_PAD:hh * HEAD_PAD + DHA] = zk[:, hh * DHA:(hh + 1) * DHA].astype(BF16)
        ka_ref[:, hh * HEAD_PAD + DHA:(hh + 1) * HEAD_PAD] = aug_k.astype(BF16)

    zbv = seg(B_V)
    mu = jnp.mean(zbv, axis=-1, keepdims=True)
    xc = zbv - mu
    var = jnp.mean(xc * xc, axis=-1, keepdims=True)
    vn = xc * lax.rsqrt(var + EPS) * lng_ref[...] + lnb_ref[...]
    vn_ref[...] = vn
    vnb = vn.astype(BF16)
    ug = seg(B_U) * _silu(seg(B_G))
    for ci in range(tm // chunk):
        rows = slice(ci * chunk, (ci + 1) * chunk)
        for g in range(GB):
            cols = slice(g * GC, (g + 1) * GC)
            mix = _dot(wc_ref[g], vnb[rows, cols]) + bc_ref[:, cols]
            ob_ref[rows, cols] = (ug[rows, cols] * mix).astype(BF16)


def _front_a(h, w, fb, lng, lnb, wc, bc, tri, *, tm, tiles_per_batch, layer):
    m, d = h.shape
    chunk = wc.shape[1]
    row = lambda width: pl.BlockSpec((tm, width), lambda i: (i, 0))
    outs = [
        (4 * HEAD_PAD, BF16), (4 * HEAD_PAD, BF16), (WA, BF16), (WA, BF16), (WA, F32), (WA, F32),
        (LANE, F32), (LANE, F32), (WB, BF16), (WB, F32)]
    vt_spec, vt_shape = _vt_specs(m, tm, HA)
    return pl.pallas_call(
        functools.partial(_front_a_kernel, tiles_per_batch=tiles_per_batch, chunk=chunk),
        grid=(m // tm,),
        in_specs=[row(d), _layer_spec(w, layer), _const_spec((1, LANE)), _const_spec((1, WB)),
                  _const_spec((1, WB)), _const_spec(wc.shape), _const_spec(bc.shape), _const_spec(tri.shape)],
        out_specs=[row(wd) for wd, _ in outs] + [vt_spec],
        out_shape=[jax.ShapeDtypeStruct((m, wd), dt) for wd, dt in outs] + [vt_shape],
        scratch_shapes=[pltpu.VMEM((8, LANE), F32)],
        compiler_params=_params("arbitrary"),
        name="front_a",
    )(h, w, fb, lng, lnb, wc, bc, tri)


def _front_b_kernel(h_ref, w_ref, qg_ref, kvg_ref, wq_ref, wqs_ref, wkv_ref, cos_ref, sin_ref,
                    inj1_ref, inj2_ref, cw_ref,
                    qm_ref, km_ref, vc_ref, gc_ref, ckv32_ref, kr32_ref, od_ref, cin32_ref, vt_ref,
                    xp_ref, *, seq_len):
    i = pl.program_id(0)
    tm = h_ref.shape[0]
    h = h_ref[...]

    def seg(off, width=512):
        return _dot(h, w_ref[:, off:off + width])

    def rms(x, g):
        return x * lax.rsqrt(jnp.mean(x * x, axis=-1, keepdims=True) + EPS) * g

    cos = cos_ref[...]
    sin = sin_ref[...]
    scale = (NOPE + ROPE_DIM) ** -0.5 * LOG2E

    cq = rms(seg(C_Q, Q_RANK), qg_ref[...]).astype(BF16)
    qf = _dot(cq, wq_ref[...])
    qs = _dot(cq, wqs_ref[...])
    for hh in range(HC):
        o = hh * HEAD_PAD
        qm_ref[:, o:o + NOPE] = (qf[:, o:o + NOPE] * scale).astype(BF16)
        rope = qf[:, o + NOPE:o + HEAD_PAD] * cos + qs[:, hh * LANE:(hh + 1) * LANE] * sin
        qm_ref[:, o + NOPE:o + HEAD_PAD] = (rope * scale).astype(BF16)

    ckv = rms(seg(C_KV, KV_RANK), kvg_ref[...])
    ckv32_ref[...] = ckv
    kv = _dot(ckv.astype(BF16), wkv_ref[...])
    kr_raw = seg(C_KR, LANE)
    lane = lax.broadcasted_iota(jnp.int32, (tm, LANE), 1)
    half = ROPE_DIM // 2
    kr_swap = jnp.where(lane < half, pltpu.roll(kr_raw, LANE - half, axis=1), pltpu.roll(kr_raw, half, axis=1))
    kr = kr_raw * cos + kr_swap * sin
    kr32_ref[...] = kr
    krb = kr.astype(BF16)
    for hh in range(HC):
        o = hh * HEAD_PAD
        km_ref[:, o:o + NOPE] = kv[:, o:o + NOPE].astype(BF16)
        km_ref[:, o + NOPE:o + HEAD_PAD] = krb
    vcs = jnp.concatenate([kv[:, hh * HEAD_PAD + NOPE:(hh + 1) * HEAD_PAD] for hh in range(HC)], axis=1)
    vc_ref[...] = vcs.astype(BF16)
    _store_values_transposed(vt_ref, vcs)
    gc_ref[...] = _silu(seg(C_G)).astype(BF16)

    @pl.when(i == 0)
    def _():
        xp_ref[0:8, :] = jnp.zeros((8, WD), F32)

    cin = seg(D_C) * seg(D_H)
    cin32_ref[...] = cin
    xp_ref[8:8 + tm, :] = cin
    s1 = xp_ref[7:7 + tm, :]
    s2 = xp_ref[6:6 + tm, :]
    t_local = (i * tm + lax.broadcasted_iota(jnp.int32, (tm, WD), 0)) % seq_len
    s1 = jnp.where(t_local == 0, inj1_ref[...], s1)
    s2 = jnp.where(t_local < 2, inj2_ref[...], s2)
    conv = s2 * cw_ref[0:1, :] + s1 * cw_ref[1:2, :] + cin * cw_ref[2:3, :]
    od_ref[...] = (seg(D_B) * conv * _silu(seg(D_G))).astype(BF16)
    xp_ref[0:8, :] = xp_ref[tm:tm + 8, :]


def _front_b(h, w, qg, kvg, wq, wqs, wkv, cos, sin, inj1, inj2, cw, *, tm, seq_len, layer):
    m, d = h.shape
    row = lambda width: pl.BlockSpec((tm, width), lambda i: (i, 0))
    n_pos = cos.shape[0] // tm
    n_inj = inj1.shape[0] // tm
    pos_spec = pl.BlockSpec((tm, LANE), lambda i: (i % n_pos, 0))
    inj_spec = pl.BlockSpec((tm, WD), lambda i: (i % n_inj, 0))
    outs = [(4 * HEAD_PAD, BF16), (4 * HEAD_PAD, BF16), (WC, BF16), (WC, BF16), (KV_RANK, F32),
            (LANE, F32), (WD, BF16), (WD, F32)]
    vt_spec, vt_shape = _vt_specs(m, tm, HC)
    return pl.pallas_call(
        functools.partial(_front_b_kernel, seq_len=seq_len),
        grid=(m // tm,),
        in_specs=[row(d), _layer_spec(w, layer), _const_spec((1, Q_RANK)), _const_spec((1, KV_RANK)),
                  _const_spec(wq.shape), _const_spec(wqs.shape), _const_spec(wkv.shape),
                  pos_spec, pos_spec, inj_spec, inj_spec, _const_spec(cw.shape)],
        out_specs=[row(wd) for wd, _ in outs] + [vt_spec],
        out_shape=[jax.ShapeDtypeStruct((m, wd), dt) for wd, dt in outs] + [vt_shape],
        scratch_shapes=[pltpu.VMEM((tm + 8, WD), F32)],
        compiler_params=_params("arbitrary"),
        name="front_b",
    )(h, w, qg, kvg, wq, wqs, wkv, cos, sin, inj1, inj2, cw)


FLASH_HEADS = 2


def _flash_kernel(q_ref, k_ref, vt_ref, g_ref, bias_ref, o_ref,
                  s_ref, p_ref, pm_ref, a_ref, m_ref, acc_ref, *, tq, tk, hp):
    qi = pl.program_id(2)
    nk = k_ref.shape[0] // tk
    r = tq // tk
    nt = (((1,), (1,)), ((), ()))

    def qk(j, slot):
        jc = jnp.minimum(j, nk - 1)
        bias = jnp.concatenate([bias_ref[jnp.clip(j - (r * qi + u) + 1, 0, 2)] for u in range(r)], axis=1)
        for h in range(hp):
            kb = k_ref[pl.ds(pl.multiple_of(jc * tk, tk), tk), h * HEAD_PAD:(h + 1) * HEAD_PAD]
            q = q_ref[:, h * HEAD_PAD:(h + 1) * HEAD_PAD]
            s = lax.dot_general(kb, q, nt, preferred_element_type=F32) + bias
            s_ref[slot, h] = s
            pm = s[0:8, :]
            for c in range(1, tk // 8):
                pm = jnp.maximum(pm, s[c * 8:(c + 1) * 8, :])
            pm_ref[slot, h] = pm

    def softmax(slot):
        for h in range(hp):
            m_old = m_ref[h]
            m_new = jnp.maximum(m_old, jnp.max(pm_ref[slot, h], axis=0, keepdims=True))
            a_ref[slot, h] = jnp.exp2(m_old - m_new)
            m_ref[h] = m_new
            p_ref[slot, h] = jnp.exp2(s_ref[slot, h] - m_new).astype(BF16)

    def pv(j, slot):
        jc = jnp.clip(j, 0, nk - 1)
        for h in range(hp):
            acc_ref[h] = a_ref[slot, h] * acc_ref[h] + _dot(vt_ref[jc, h], p_ref[slot, h])

    m_ref[...] = jnp.full_like(m_ref, NEG_INF)
    acc_ref[...] = jnp.zeros_like(acc_ref)
    p_ref[1] = jnp.zeros_like(p_ref[1])
    a_ref[1] = jnp.ones_like(a_ref[1])
    qk(0, 0)

    def pair(jj, c):
        j = 2 * jj
        qk(j + 1, 1)
        softmax(0)
        pv(j - 1, 1)
        qk(j + 2, 0)
        softmax(1)
        pv(j, 0)
        return c

    n_pairs = (r * (qi + 1) + 1) // 2
    lax.fori_loop(0, n_pairs, pair, 0)
    pv(2 * n_pairs - 1, 1)
    for h in range(hp):
        acc = acc_ref[h]
        o = (acc[:VDIM, :] / acc[VDIM:VDIM + 1, :]).T
        cols = slice(h * VDIM, (h + 1) * VDIM)
        o_ref[:, cols] = (o * g_ref[:, cols].astype(F32)).astype(o_ref.dtype)


def _flash(q, k, vt, g, bias, *, batch, seq_len, heads, tq, name):
    tk = vt.shape[3]
    nq, nk = seq_len // tq, seq_len // tk
    hp = FLASH_HEADS
    q_spec = pl.BlockSpec((tq, hp * HEAD_PAD), lambda b, h, i: (b * nq + i, h))
    o_spec = pl.BlockSpec((tq, hp * VDIM), lambda b, h, i: (b * nq + i, h))
    once = pl.Buffered(1)
    return pl.pallas_call(
        functools.partial(_flash_kernel, tq=tq, tk=tk, hp=hp),
        grid=(batch, heads // hp, nq),
        in_specs=[q_spec,
                  pl.BlockSpec((seq_len, hp * HEAD_PAD), lambda b, h, i: (b, h), pipeline_mode=once),
                  pl.BlockSpec((nk, hp, DV_PAD, tk), lambda b, h, i: (b, h, 0, 0), pipeline_mode=once),
                  o_spec,
                  _const_spec(bias.shape)],
        out_specs=o_spec,
        out_shape=jax.ShapeDtypeStruct(g.shape, BF16),
        scratch_shapes=[pltpu.VMEM((2, hp, tk, tq), F32), pltpu.VMEM((2, hp, tk, tq), BF16),
                        pltpu.VMEM((2, hp, 8, tq), F32), pltpu.VMEM((2, hp, 1, tq), F32),
                        pltpu.VMEM((hp, 1, tq), F32), pltpu.VMEM((hp, DV_PAD, tq), F32)],
        compiler_params=_params("parallel", "parallel", "arbitrary"),
        name=name,
    )(q, k, vt, g, bias)


def _mask_bias(t, gran):
    key = np.arange(t)[:, None] // gran
    qry = np.arange(t)[None, :] // gran
    diag = np.where(key <= qry, 0.0, NEG_INF)
    return jnp.asarray(np.stack([np.zeros((t, t)), diag, np.full((t, t), NEG_INF)]), F32)


def _sample_attn_kernel(qa_ref, ka_ref, v_ref, ga_ref, cumc_ref, cumr_ref, ck_ref, cv_ref, clogf_ref, utri_ref,
                        qm_ref, km_ref, vc_ref, gc_ref, cckv_ref, ckr_ref, wkv_ref,
                        oa_ref, oc_ref, *, past_len):
    ts = qa_ref.shape[0]
    row = lax.broadcasted_iota(jnp.int32, (ts, ts), 0)
    col = lax.broadcasted_iota(jnp.int32, (ts, ts), 1)
    nt = (((1,), (1,)), ((), ()))

    def attend(s_c, s_n, v_c, v_n):
        m = jnp.maximum(jnp.max(s_c, axis=-1, keepdims=True), jnp.max(s_n, axis=-1, keepdims=True))
        p_c = jnp.exp2(s_c - m)
        p_n = jnp.exp2(s_n - m)
        l = jnp.sum(p_c, axis=-1, keepdims=True) + jnp.sum(p_n, axis=-1, keepdims=True)
        return (_dot(p_c.astype(BF16), v_c) + _dot(p_n.astype(BF16), v_n)) / l

    utri = utri_ref[...]
    hi, mid, lo = _split3(clogf_ref[...])
    c_cache = ((_dot(hi, utri) + _dot(mid, utri)) + _dot(lo, utri)) * LOG2E
    cum_col = cumc_ref[...]
    cum_row = cumr_ref[...]
    causal = col <= row
    for hh in range(HA):
        q = qa_ref[:, hh * HEAD_PAD:hh * HEAD_PAD + DHA]
        kn = ka_ref[:, hh * HEAD_PAD:hh * HEAD_PAD + DHA]
        kc = ck_ref[:, hh * DHA:(hh + 1) * DHA].astype(BF16)
        vcache = cv_ref[:, hh * DHA:(hh + 1) * DHA].astype(BF16)
        base = c_cache[hh:hh + 1, past_len - 1:past_len]
        cq = cum_col[:, hh:hh + 1] + base
        s_c = lax.dot_general(q, kc, nt, preferred_element_type=F32) + (cq - c_cache[hh:hh + 1, :])
        s_n = lax.dot_general(q, kn, nt, preferred_element_type=F32) + (cq - (cum_row[hh:hh + 1, :] + base))
        s_n = jnp.where(causal, s_n, NEG_INF)
        o = attend(s_c, s_n, vcache, v_ref[:, hh * DHA:(hh + 1) * DHA])
        oa_ref[:, hh * DHA:(hh + 1) * DHA] = (o * ga_ref[:, hh * DHA:(hh + 1) * DHA].astype(F32)).astype(BF16)

    kv_c = _dot(cckv_ref[...].astype(BF16), wkv_ref[...])
    ckr = ckr_ref[...]
    chunk_ok = (past_len + col) // CHUNK <= (past_len + row) // CHUNK
    for hh in range(HC):
        o_ = hh * HEAD_PAD
        qn = qm_ref[:, o_:o_ + NOPE]
        qr = qm_ref[:, o_ + NOPE:o_ + HEAD_PAD]
        s_c = (lax.dot_general(qn, kv_c[:, o_:o_ + NOPE].astype(BF16), nt, preferred_element_type=F32)
               + lax.dot_general(qr, ckr, nt, preferred_element_type=F32))
        s_n = lax.dot_general(qm_ref[:, o_:o_ + HEAD_PAD], km_ref[:, o_:o_ + HEAD_PAD], nt,
                              preferred_element_type=F32)
        s_n = jnp.where(chunk_ok, s_n, NEG_INF)
        o = attend(s_c, s_n, kv_c[:, o_ + NOPE:o_ + HEAD_PAD].astype(BF16), vc_ref[:, hh * VDIM:(hh + 1) * VDIM])
        oc_ref[:, hh * VDIM:(hh + 1) * VDIM] = (o * gc_ref[:, hh * VDIM:(hh + 1) * VDIM].astype(F32)).astype(BF16)


def _sample_attn(qa, ka, v, ga, cum_col, cum_row, ck, cv, clogf, utri, qm, km, vc, gc, cckv, ckr, wkv,
                 *, batch, ts, past_len, layer):
    row = lambda width: pl.BlockSpec((ts, width), lambda b: (b, 0))
    per_b = lambda shape: pl.BlockSpec((None,) + shape, lambda b: (b, 0, 0))
    cache = lambda width: pl.BlockSpec((None, past_len, width), lambda b: (layer * batch + b, 0, 0))
    return pl.pallas_call(
        functools.partial(_sample_attn_kernel, past_len=past_len),
        grid=(batch,),
        in_specs=[row(4 * HEAD_PAD), row(4 * HEAD_PAD), row(WA), row(WA), row(LANE), per_b((8, ts)),
                  cache(WA), cache(WA), per_b((8, past_len)), _const_spec(utri.shape),
                  row(4 * HEAD_PAD), row(4 * HEAD_PAD), row(WC), row(WC),
                  cache(KV_RANK), per_b((past_len, LANE)), _const_spec(wkv.shape)],
        out_specs=[row(WA), row(WC)],
        out_shape=[jax.ShapeDtypeStruct((batch * ts, WA), BF16), jax.ShapeDtypeStruct((batch * ts, WC), BF16)],
        compiler_params=_params("parallel"),
        name="sample_attn",
    )(qa, ka, v, ga, cum_col, cum_row, ck, cv, clogf, utri, qm, km, vc, gc, cckv, ckr, wkv)


def _merge_kernel(h_ref, oa_ref, ob_ref, oc_ref, od_ref, wga_ref, wgb_ref, wgc_ref, wgd_ref, wb_ref, mg_ref):
    h = h_ref[...]
    merged = None
    for br, (o_ref, wg_ref) in enumerate(zip((oa_ref, ob_ref, oc_ref, od_ref),
                                             (wga_ref, wgb_ref, wgc_ref, wgd_ref))):
        term = jax.nn.sigmoid(_dot(h, wg_ref[...])) * _dot(o_ref[...], wb_ref[br])
        merged = term if merged is None else merged + term
    mg_ref[...] = merged.astype(mg_ref.dtype)


def _merge(h, oa, ob, oc, od, wg, wb, *, tm, tj, layer):
    m, d = h.shape
    nj = d // tj
    row = lambda width: pl.BlockSpec((tm, width), lambda j, i: (i, 0))
    gate = lambda br: pl.BlockSpec((None, d, tj), lambda j, i: (layer, 0, br * nj + j))
    return pl.pallas_call(
        _merge_kernel,
        grid=(nj, m // tm),
        in_specs=[row(d), row(WA), row(WB), row(WC), row(WD), gate(0), gate(1), gate(2), gate(3),
                  pl.BlockSpec((N_BRANCH, WA, tj), lambda j, i: (0, 0, j))],
        out_specs=pl.BlockSpec((tm, tj), lambda j, i: (i, j)),
        out_shape=jax.ShapeDtypeStruct((m, d), BF16),
        compiler_params=_params("parallel", "parallel"),
        name="merge",
    )(h, oa, ob, oc, od, wg, wg, wg, wg, wb)


def _out_proj_kernel(mg_ref, x_ref, wo_ref, gn_ref, *out_refs):
    x = x_ref[...] + _dot(mg_ref[...], wo_ref[...])
    y = x * lax.rsqrt(jnp.mean(x * x, axis=-1, keepdims=True) + EPS)
    hn_ref = out_refs[-1]
    hn_ref[...] = (y * gn_ref[...]).astype(hn_ref.dtype)
    if len(out_refs) == 2:
        out_refs[0][...] = x


def _out_proj(mg, x, wo, gn, *, tm, norm_dtype, want_x):
    m, d = x.shape
    row = pl.BlockSpec((tm, d), lambda i: (i, 0))
    shapes = [jax.ShapeDtypeStruct((m, d), norm_dtype)]
    if want_x:
        shapes = [jax.ShapeDtypeStruct((m, d), F32)] + shapes
    outs = pl.pallas_call(
        _out_proj_kernel,
        grid=(m // tm,),
        in_specs=[row, row, _const_spec(wo.shape), _const_spec((1, d))],
        out_specs=[row] * len(shapes),
        out_shape=shapes,
        compiler_params=_params("parallel"),
        name="out_proj",
    )(mg, x, wo, gn)
    return (outs[0], outs[1]) if want_x else (None, outs[0])


def _rope_tables(pos):
    half = ROPE_DIM // 2
    inv = ROPE_BASE ** (-jnp.arange(half, dtype=F32) / half)
    ang = pos.astype(F32)[:, None] * inv[None, :]
    z = jnp.zeros((pos.shape[0], LANE - ROPE_DIM), F32)
    cos = jnp.concatenate([jnp.cos(ang), jnp.cos(ang), z], axis=-1)
    sin = jnp.concatenate([-jnp.sin(ang), jnp.sin(ang), z], axis=-1)
    return cos, sin


def _pad_cols(a, width):
    return jnp.pad(a, ((0, 0), (0, width - a.shape[1])))


def _repack_kernel(tab_ref, w_ref, o_ref):
    keep = tab_ref[1, pl.program_id(0)]
    rows = lax.broadcasted_iota(jnp.int32, (LANE, w_ref.shape[2]), 0)
    for l in range(w_ref.shape[1]):
        y = jnp.where(rows < keep, w_ref[:, l, :], 0.0)
        o_ref[l] = y.T.astype(BF16)


def _repack(wt, table):
    n_in, depth, d = wt.shape
    assert all(0 <= s and s + LANE <= n_in for s, _ in table)
    tab = jnp.asarray(np.asarray(table, np.int32).T)
    return pl.pallas_call(
        _repack_kernel,
        grid_spec=pltpu.PrefetchScalarGridSpec(
            num_scalar_prefetch=1, grid=(len(table),),
            in_specs=[pl.BlockSpec((pl.Element(LANE), pl.Element(depth), pl.Element(d)),
                                   lambda i, t: (t[0, i], 0, 0))],
            out_specs=pl.BlockSpec((depth, d, LANE), lambda i, t: (0, 0, i))),
        out_shape=jax.ShapeDtypeStruct((depth, d, LANE * len(table)), BF16),
        compiler_params=_params("arbitrary"),
        name="repack",
    )(tab, wt)


def _repack_tables(d):
    sizes = (WA, WA, WA, HA, WA, WB, WB, WB, Q_RANK, KV_RANK, ROPE_DIM, WC, WD, WD, WD, WD, N_BRANCH * d)
    (a_q, _, _, a_f, a_g, _, _, _, c_q, _, c_kr, c_g, d_b, _, _, _, m_g, end) = (
        int(o) for o in np.concatenate([[0], np.cumsum(sizes)]))
    run = lambda start, stop: [(s, LANE) for s in range(start, stop, LANE)]
    ta = run(a_q, a_f) + run(a_g, c_q) + [(a_f, HA)]
    tb = run(c_q, c_kr) + run(c_g, d_b) + [(c_kr, ROPE_DIM)] + run(d_b, m_g)
    tg = run(m_g, end)
    assert len(ta) * LANE == A_COLS and len(tb) * LANE == B_COLS
    return ta, tb, tg


def _layer_weights(l, d, fox_fb, gm_ln_g, gm_ln_b, mla_q_norm_g, mla_wq_b, mla_kv_norm_g, mla_wkv_b,
                   conv_w, w_branch, w_out):
    half = ROPE_DIM // 2
    swap = lambda a: jnp.concatenate([a[..., half:], a[..., :half]], axis=-1)
    wq = mla_wq_b[l].reshape(Q_RANK, HC, NOPE + ROPE_DIM)
    zpad = jnp.zeros((Q_RANK, HC, LANE - ROPE_DIM), F32)
    wq2 = jnp.concatenate([wq, zpad], axis=-1).reshape(Q_RANK, HC * HEAD_PAD).astype(BF16)
    wqs = jnp.concatenate([swap(wq[..., NOPE:]), zpad], axis=-1).reshape(Q_RANK, HC * LANE).astype(BF16)
    return dict(
        wq=wq2, wqs=wqs, wkv=mla_wkv_b[l].astype(BF16),
        fb=_pad_cols(fox_fb[l].reshape(1, HA), LANE),
        lng=gm_ln_g[l].reshape(1, WB), lnb=gm_ln_b[l].reshape(1, WB),
        qg=mla_q_norm_g[l].reshape(1, Q_RANK), kvg=mla_kv_norm_g[l].reshape(1, KV_RANK),
        cw=jnp.pad(conv_w[l], ((0, 8 - CONV_W), (0, 0))),
        wbr=w_branch[l].astype(BF16), wo=w_out[l].astype(BF16))


def _group_layer(lw, wts, layer, x, h, gn, norm_dtype, want_x, *, batch, seq_len, tm, cos, sin, inj1, inj2, wc, bc, tri, attn):
    tiles_per_batch = max(seq_len // tm, 1)
    wa_all, wb_all, wg_all = wts
    fa = _front_a(h, wa_all, lw["fb"], lw["lng"], lw["lnb"], wc, bc, tri, tm=tm, tiles_per_batch=tiles_per_batch,
                  layer=layer)
    fb = _front_b(h, wb_all, lw["qg"], lw["kvg"], lw["wq"], lw["wqs"], lw["wkv"], cos, sin, inj1, inj2,
                  lw["cw"], tm=tm, seq_len=seq_len, layer=layer)
    qa, ka, v, ga, k32, v32, logf, cum, ob, vn, vta = fa
    qm, km, vc, gc, ckv32, kr32, od, cin32, vtc = fb
    oa, oc = attn(fa, fb)
    m, d = x.shape
    tm_merge = 2 * tm if m % (2 * tm) == 0 else tm
    mg = _merge(h, oa, ob, oc, od, wg_all, lw["wbr"], tm=tm_merge, tj=min(512, d), layer=layer)
    x_new, normed = _out_proj(mg, x, lw["wo"], gn.reshape(1, d), tm=tm, norm_dtype=norm_dtype, want_x=want_x)
    r3 = lambda a: a.reshape(batch, seq_len, a.shape[-1])
    states = dict(k=r3(k32), v=r3(v32), logf=r3(logf)[..., :HA], ckv=r3(ckv32), kr=r3(kr32)[..., :ROPE_DIM],
                  cin=r3(cin32), vn=r3(vn))
    return x_new, normed, states


def kernel(x_prompt, x_sample, cache_fox_k, cache_fox_v, cache_fox_logf, cache_mla_ckv, cache_mla_krope, state_conv, norm_g, w_in, fox_fb, gm_ln_g, gm_ln_b, gm_ws, gm_bs, mla_q_norm_g, mla_wq_b, mla_kv_norm_g, mla_wkv_b, conv_w, w_branch, w_out, final_norm_g):
    bp, tp, d = x_prompt.shape
    bs, ts, _ = x_sample.shape
    depth = w_in.shape[0]
    past = cache_fox_k.shape[2]
    keep = min(tp, past)
    tm_p = min(512, tp)
    tm_s = bs * ts
    t_attn = tm_p
    tq_attn = 2 * t_attn if tp % (2 * t_attn) == 0 else t_attn
    assert tp % tm_p == 0 and tm_p % GM_CHUNK == 0 and t_attn % CHUNK == 0
    assert ts <= GM_CHUNK and tm_s % 8 == 0 and past % CHUNK == 0

    xp = x_prompt.reshape(bp * tp, d)
    xs = x_sample.reshape(bs * ts, d)

    cos_p, sin_p = _rope_tables(jnp.arange(tp, dtype=jnp.int32))
    cos_s, sin_s = _rope_tables(past + jnp.arange(ts, dtype=jnp.int32))
    cos_s, sin_s = jnp.tile(cos_s, (bs, 1)), jnp.tile(sin_s, (bs, 1))
    ar_p = np.arange(tm_p)
    tri_p = jnp.asarray(ar_p[None, :] <= ar_p[:, None], BF16)
    ar_s = np.arange(tm_s)
    same_b = (ar_s[None, :] // ts) == (ar_s[:, None] // ts)
    tri_s = jnp.asarray(same_b & (ar_s[None, :] <= ar_s[:, None]), BF16)
    ar_c = np.arange(past)
    utri_c = jnp.asarray(ar_c[:, None] <= ar_c[None, :], BF16)
    zero_inj = jnp.zeros((tm_p, WD), F32)
    tril = np.tril(np.ones((GM_CHUNK, GM_CHUNK), bool))
    bias_fox = _mask_bias(t_attn, 1)
    bias_mla = _mask_bias(t_attn, CHUNK)

    wt = jnp.transpose(w_in, (2, 0, 1))
    wts = tuple(_repack(wt, tab) for tab in _repack_tables(d))

    hp = _rmsnorm(xp, norm_g[0], tm_p, BF16)
    hs = _rmsnorm(xs, norm_g[0], tm_s, BF16)
    st_p, st_s = [], []
    for l in range(depth):
        lw = _layer_weights(l, d, fox_fb, gm_ln_g, gm_ln_b, mla_q_norm_g, mla_wq_b, mla_kv_norm_g,
                            mla_wkv_b, conv_w, w_branch, w_out)
        last = l == depth - 1
        gn = final_norm_g if last else norm_g[l + 1]
        norm_dtype = F32 if last else BF16

        wm = jnp.where(tril, gm_ws[l], 0.0)
        wc_p = wm.astype(BF16)
        bc_p = jnp.repeat(gm_bs[l].T, GC, axis=1)
        wc_s = jnp.einsum("ab,gts->gatbs", jnp.eye(bs, dtype=F32), wm[:, :ts, :ts]).reshape(
            GB, tm_s, tm_s).astype(BF16)
        bc_s = jnp.tile(bc_p[:ts], (bs, 1))

        def attn_p(fa, fb):
            oa = _flash(fa[0], fa[1], fa[10], fa[3], bias_fox, batch=bp, seq_len=tp, heads=HA, tq=tq_attn,
                        name="flash_fox")
            oc = _flash(fb[0], fb[1], fb[8], fb[3], bias_mla, batch=bp, seq_len=tp, heads=HC, tq=tq_attn,
                        name="flash_mla")
            return oa, oc

        xp, hp, sp = _group_layer(lw, wts, l, xp, hp, gn, norm_dtype, not last, batch=bp, seq_len=tp, tm=tm_p,
                                  cos=cos_p, sin=sin_p, inj1=zero_inj, inj2=zero_inj, wc=wc_p, bc=bc_p,
                                  tri=tri_p, attn=attn_p)

        sc = state_conv[l]
        inj1 = jnp.zeros((bs, ts, WD), F32).at[:, 0].set(sc[:, 1]).reshape(tm_s, WD)
        inj2 = jnp.zeros((bs, ts, WD), F32).at[:, 0].set(sc[:, 0]).at[:, 1].set(sc[:, 1]).reshape(tm_s, WD)
        clogf = jnp.pad(jnp.transpose(cache_fox_logf[l], (0, 2, 1)), ((0, 0), (0, 8 - HA), (0, 0)))
        ckr = jnp.pad(cache_mla_krope[l], ((0, 0), (0, 0), (0, LANE - ROPE_DIM))).astype(BF16)

        def attn_s(fa, fb):
            cum_row = jnp.transpose(fa[7].reshape(bs, ts, LANE)[:, :, :8], (0, 2, 1))
            return _sample_attn(fa[0], fa[1], fa[2], fa[3], fa[7], cum_row,
                                cache_fox_k.reshape(depth * bs, past, WA), cache_fox_v.reshape(depth * bs, past, WA),
                                clogf, utri_c, fb[0], fb[1], fb[2], fb[3],
                                cache_mla_ckv.reshape(depth * bs, past, KV_RANK), ckr, lw["wkv"],
                                batch=bs, ts=ts, past_len=past, layer=l)

        xs, hs, ss = _group_layer(lw, wts, l, xs, hs, gn, norm_dtype, not last, batch=bs, seq_len=ts, tm=tm_s,
                                  cos=cos_s, sin=sin_s, inj1=inj1, inj2=inj2, wc=wc_s, bc=bc_s, tri=tri_s,
                                  attn=attn_s)
        st_p.append(sp)
        st_s.append(ss)

    y_prompt = hp.reshape(bp, tp, d)
    y_sample = hs.reshape(bs, ts, d)
    stk = lambda sts, f: jnp.stack([f(s) for s in sts])
    p_out = (stk(st_p, lambda s: s["k"][:, tp - keep:].reshape(bp, keep, HA, DHA)),
             stk(st_p, lambda s: s["v"][:, tp - keep:].reshape(bp, keep, HA, DHA)),
             stk(st_p, lambda s: s["logf"][:, tp - keep:]),
             stk(st_p, lambda s: s["ckv"][:, tp - keep:]),
             stk(st_p, lambda s: s["kr"][:, tp - keep:]),
             stk(st_p, lambda s: s["cin"][:, tp - (CONV_W - 1):]),
             stk(st_p, lambda s: s["vn"][:, tp - GM_CHUNK:]))
    s_out = (stk(st_s, lambda s: s["k"].reshape(bs, ts, HA, DHA)),
             stk(st_s, lambda s: s["v"].reshape(bs, ts, HA, DHA)),
             stk(st_s, lambda s: s["logf"]),
             stk(st_s, lambda s: s["ckv"]),
             stk(st_s, lambda s: s["kr"]),
             stk(st_s, lambda s: s["cin"][:, ts - (CONV_W - 1):]),
             stk(st_s, lambda s: s["vn"]))
    return (y_prompt, y_sample) + p_out + s_out
```

```python
import functools

import jax
import jax.numpy as jnp
import numpy as np
from jax import lax
from jax.experimental import pallas as pl
from jax.experimental.pallas import tpu as pltpu

F32 = jnp.float32
BF16 = jnp.bfloat16

N_BRANCH = 4
HA, DHA = 4, 128
WA = HA * DHA
WB, GB = 512, 4
GC = WB // GB
GM_CHUNK = 128
HC, NOPE, ROPE_DIM, VDIM = 4, 128, 64, 128
Q_RANK, KV_RANK = 384, 256
WC = HC * VDIM
WD = 512
CONV_W = 3
CHUNK = 64
ROPE_BASE = 10000.0
EPS = 1e-6
NEG_INF = -1e30

LOG2E = 1.4426950408889634
LANE = 128
HEAD_PAD = 2 * LANE
DV_PAD = VDIM + 16
VMEM_LIMIT = 56 * 1024 * 1024

A_Q, A_K, A_V, A_G, B_U, B_V, B_G, A_F = 0, 512, 1024, 1536, 2048, 2560, 3072, 3584
A_COLS = A_F + LANE
C_Q, C_KV, C_G, C_KR = 0, 384, 640, 1152
D_B, D_C, D_H, D_G = 1280, 1792, 2304, 2816
B_COLS = D_G + WD


def _dot(a, b):
    return jnp.dot(a, b, preferred_element_type=F32)


def _silu(x):
    return x * jax.nn.sigmoid(x)


def _split3(x):
    hi = x.astype(BF16)
    r = x - hi.astype(F32)
    mid = r.astype(BF16)
    lo = (r - mid.astype(F32)).astype(BF16)
    return hi, mid, lo


def _store_values_transposed(vt_ref, v):
    rows = v.shape[0]
    sub = lax.broadcasted_iota(jnp.int32, (DV_PAD - VDIM, rows), 0)
    ones_row = jnp.where(sub == 0, 1.0, 0.0).astype(BF16)
    for hh in range(v.shape[1] // VDIM):
        vt_ref[0, hh, 0:VDIM, :] = v[:, hh * VDIM:(hh + 1) * VDIM].T.astype(BF16)
        vt_ref[0, hh, VDIM:DV_PAD, :] = ones_row


def _vt_specs(m, tm, heads):
    return (pl.BlockSpec((1, heads, DV_PAD, tm), lambda i: (i, 0, 0, 0)),
            jax.ShapeDtypeStruct((m // tm, heads, DV_PAD, tm), BF16))


def _layer_spec(w, layer):
    return pl.BlockSpec((None,) + w.shape[1:], lambda *_: (layer, 0, 0), pipeline_mode=pl.Buffered(1))


def _const_spec(shape):
    nd = len(shape)
    return pl.BlockSpec(shape, lambda *_: (0,) * nd, pipeline_mode=pl.Buffered(1))


def _params(*sem):
    return pltpu.CompilerParams(dimension_semantics=sem, vmem_limit_bytes=VMEM_LIMIT)


def _rmsnorm_kernel(x_ref, g_ref, o_ref):
    x = x_ref[...]
    y = x * lax.rsqrt(jnp.mean(x * x, axis=-1, keepdims=True) + EPS)
    o_ref[...] = (y * g_ref[...]).astype(o_ref.dtype)


def _rmsnorm(x, g, tm, out_dtype):
    m, d = x.shape
    return pl.pallas_call(
        _rmsnorm_kernel,
        grid=(m // tm,),
        in_specs=[pl.BlockSpec((tm, d), lambda i: (i, 0)), _const_spec((1, d))],
        out_specs=pl.BlockSpec((tm, d), lambda i: (i, 0)),
        out_shape=jax.ShapeDtypeStruct((m, d), out_dtype),
        compiler_params=_params("parallel"),
        name="rmsnorm",
    )(x, g.reshape(1, d))


def _front_a_kernel(h_ref, w_ref, fb_ref, lng_ref, lnb_ref, wc_ref, bc_ref, tri_ref,
                    qa_ref, ka_ref, v_ref, ga_ref, k32_ref, v32_ref, logf_ref, cum_ref, ob_ref, vn_ref, vt_ref,
                    carry_ref, *, tiles_per_batch, chunk):
    i = pl.program_id(0)
    tm = h_ref.shape[0]
    h = h_ref[...]

    def seg(off, width=512):
        return _dot(h, w_ref[:, off:off + width])

    x = seg(A_F, LANE) + fb_ref[...]
    logf = jnp.minimum(x, 0.0) - jnp.log1p(jnp.exp(-jnp.abs(x)))
    logf_ref[...] = logf

    @pl.when(i % tiles_per_batch == 0)
    def _():
        carry_ref[...] = jnp.zeros_like(carry_ref)

    tri = tri_ref[...]
    hi, mid, lo = _split3(logf)
    cum = (_dot(tri, hi) + _dot(tri, mid)) + _dot(tri, lo) + carry_ref[0:1, :]
    carry_ref[...] = jnp.broadcast_to(cum[tm - 1:tm, :], carry_ref.shape)
    cum = cum * LOG2E
    cum_ref[...] = cum

    zq = seg(A_Q) * (DHA ** -0.5 * LOG2E)
    zk = seg(A_K)
    zv = seg(A_V)
    k32_ref[...] = zk
    v32_ref[...] = zv
    v_ref[...] = zv.astype(BF16)
    _store_values_transposed(vt_ref, zv)
    ga_ref[...] = _silu(seg(A_G)).astype(BF16)

    lane = lax.broadcasted_iota(jnp.int32, (tm, LANE), 1)
    c_hi = cum.astype(BF16).astype(F32)
    r = cum - c_hi
    c_mid = r.astype(BF16).astype(F32)
    c_lo = r - c_mid
    for hh in range(HA):
        bh = jnp.broadcast_to(c_hi[:, hh:hh + 1], (tm, LANE))
        bm = jnp.broadcast_to(c_mid[:, hh:hh + 1], (tm, LANE))
        bl = jnp.broadcast_to(c_lo[:, hh:hh + 1], (tm, LANE))
        aug_q = jnp.where(lane == 0, bh, jnp.where(lane == 1, bm, jnp.where(lane == 2, bl,
                          jnp.where(lane < 6, 1.0, 0.0))))
        aug_k = jnp.where(lane < 3, 1.0, jnp.where(lane == 3, -bh, jnp.where(lane == 4, -bm,
                          jnp.where(lane == 5, -bl, 0.0))))
        qa_ref[:, hh * HEAD_PAD:hh * HEAD_PAD + DHA] = zq[:, hh * DHA:(hh + 1) * DHA].astype(BF16)
        qa_ref[:, hh * HEAD_PAD + DHA:(hh + 1) * HEAD_PAD] = aug_q.astype(BF16)
        ka_ref[:, hh * HEAD_PAD:hh * HEAD_PAD + DHA] = zk[:, hh * DHA:(hh + 1) * DHA].astype(BF16)
        ka_ref[:, hh * HEAD_PAD + DHA:(hh + 1) * HEAD_PAD] = aug_k.astype(BF16)

    zbv = seg(B_V)
    mu = jnp.mean(zbv, axis=-1, keepdims=True)
    xc = zbv - mu
    var = jnp.mean(xc * xc, axis=-1, keepdims=True)
    vn = xc * lax.rsqrt(var + EPS) * lng_ref[...] + lnb_ref[...]
    vn_ref[...] = vn
    vnb = vn.astype(BF16)
    ug = seg(B_U) * _silu(seg(B_G))
    for ci in range(tm // chunk):
        rows = slice(ci * chunk, (ci + 1) * chunk)
        for g in range(GB):
            cols = slice(g * GC, (g + 1) * GC)
            mix = _dot(wc_ref[g], vnb[rows, cols]) + bc_ref[:, cols]
            ob_ref[rows, cols] = (ug[rows, cols] * mix).astype(BF16)


def _front_a(h, w, fb, lng, lnb, wc, bc, tri, *, tm, tiles_per_batch, layer):
    m, d = h.shape
    chunk = wc.shape[1]
    row = lambda width: pl.BlockSpec((tm, width), lambda i: (i, 0))
    outs = [
        (4 * HEAD_PAD, BF16), (4 * HEAD_PAD, BF16), (WA, BF16), (WA, BF16), (WA, F32), (WA, F32),
        (LANE, F32), (LANE, F32), (WB, BF16), (WB, F32)]
    vt_spec, vt_shape = _vt_specs(m, tm, HA)
    return pl.pallas_call(
        functools.partial(_front_a_kernel, tiles_per_batch=tiles_per_batch, chunk=chunk),
        grid=(m // tm,),
        in_specs=[row(d), _layer_spec(w, layer), _const_spec((1, LANE)), _const_spec((1, WB)),
                  _const_spec((1, WB)), _const_spec(wc.shape), _const_spec(bc.shape), _const_spec(tri.shape)],
        out_specs=[row(wd) for wd, _ in outs] + [vt_spec],
        out_shape=[jax.ShapeDtypeStruct((m, wd), dt) for wd, dt in outs] + [vt_shape],
        scratch_shapes=[pltpu.VMEM((8, LANE), F32)],
        compiler_params=_params("arbitrary"),
        name="front_a",
    )(h, w, fb, lng, lnb, wc, bc, tri)


def _front_b_kernel(h_ref, w_ref, qg_ref, kvg_ref, wq_ref, wqs_ref, wkv_ref, cos_ref, sin_ref,
                    inj1_ref, inj2_ref, cw_ref,
                    qm_ref, km_ref, vc_ref, gc_ref, ckv32_ref, kr32_ref, od_ref, cin32_ref, vt_ref,
                    xp_ref, *, seq_len):
    i = pl.program_id(0)
    tm = h_ref.shape[0]
    h = h_ref[...]

    def seg(off, width=512):
        return _dot(h, w_ref[:, off:off + width])

    def rms(x, g):
        return x * lax.rsqrt(jnp.mean(x * x, axis=-1, keepdims=True) + EPS) * g

    cos = cos_ref[...]
    sin = sin_ref[...]
    scale = (NOPE + ROPE_DIM) ** -0.5 * LOG2E

    cq = rms(seg(C_Q, Q_RANK), qg_ref[...]).astype(BF16)
    qf = _dot(cq, wq_ref[...])
    qs = _dot(cq, wqs_ref[...])
    for hh in range(HC):
        o = hh * HEAD_PAD
        qm_ref[:, o:o + NOPE] = (qf[:, o:o + NOPE] * scale).astype(BF16)
        rope = qf[:, o + NOPE:o + HEAD_PAD] * cos + qs[:, hh * LANE:(hh + 1) * LANE] * sin
        qm_ref[:, o + NOPE:o + HEAD_PAD] = (rope * scale).astype(BF16)

    ckv = rms(seg(C_KV, KV_RANK), kvg_ref[...])
    ckv32_ref[...] = ckv
    kv = _dot(ckv.astype(BF16), wkv_ref[...])
    kr_raw = seg(C_KR, LANE)
    lane = lax.broadcasted_iota(jnp.int32, (tm, LANE), 1)
    half = ROPE_DIM // 2
    kr_swap = jnp.where(lane < half, pltpu.roll(kr_raw, LANE - half, axis=1), pltpu.roll(kr_raw, half, axis=1))
    kr = kr_raw * cos + kr_swap * sin
    kr32_ref[...] = kr
    krb = kr.astype(BF16)
    for hh in range(HC):
        o = hh * HEAD_PAD
        km_ref[:, o:o + NOPE] = kv[:, o:o + NOPE].astype(BF16)
        km_ref[:, o + NOPE:o + HEAD_PAD] = krb
    vcs = jnp.concatenate([kv[:, hh * HEAD_PAD + NOPE:(hh + 1) * HEAD_PAD] for hh in range(HC)], axis=1)
    vc_ref[...] = vcs.astype(BF16)
    _store_values_transposed(vt_ref, vcs)
    gc_ref[...] = _silu(seg(C_G)).astype(BF16)

    @pl.when(i == 0)
    def _():
        xp_ref[0:8, :] = jnp.zeros((8, WD), F32)

    cin = seg(D_C) * seg(D_H)
    cin32_ref[...] = cin
    xp_ref[8:8 + tm, :] = cin
    s1 = xp_ref[7:7 + tm, :]
    s2 = xp_ref[6:6 + tm, :]
    t_local = (i * tm + lax.broadcasted_iota(jnp.int32, (tm, WD), 0)) % seq_len
    s1 = jnp.where(t_local == 0, inj1_ref[...], s1)
    s2 = jnp.where(t_local < 2, inj2_ref[...], s2)
    conv = s2 * cw_ref[0:1, :] + s1 * cw_ref[1:2, :] + cin * cw_ref[2:3, :]
    od_ref[...] = (seg(D_B) * conv * _silu(seg(D_G))).astype(BF16)
    xp_ref[0:8, :] = xp_ref[tm:tm + 8, :]


def _front_b(h, w, qg, kvg, wq, wqs, wkv, cos, sin, inj1, inj2, cw, *, tm, seq_len, layer):
    m, d = h.shape
    row = lambda width: pl.BlockSpec((tm, width), lambda i: (i, 0))
    n_pos = cos.shape[0] // tm
    n_inj = inj1.shape[0] // tm
    pos_spec = pl.BlockSpec((tm, LANE), lambda i: (i % n_pos, 0))
    inj_spec = pl.BlockSpec((tm, WD), lambda i: (i % n_inj, 0))
    outs = [(4 * HEAD_PAD, BF16), (4 * HEAD_PAD, BF16), (WC, BF16), (WC, BF16), (KV_RANK, F32),
            (LANE, F32), (WD, BF16), (WD, F32)]
    vt_spec, vt_shape = _vt_specs(m, tm, HC)
    return pl.pallas_call(
        functools.partial(_front_b_kernel, seq_len=seq_len),
        grid=(m // tm,),
        in_specs=[row(d), _layer_spec(w, layer), _const_spec((1, Q_RANK)), _const_spec((1, KV_RANK)),
                  _const_spec(wq.shape), _const_spec(wqs.shape), _const_spec(wkv.shape),
                  pos_spec, pos_spec, inj_spec, inj_spec, _const_spec(cw.shape)],
        out_specs=[row(wd) for wd, _ in outs] + [vt_spec],
        out_shape=[jax.ShapeDtypeStruct((m, wd), dt) for wd, dt in outs] + [vt_shape],
        scratch_shapes=[pltpu.VMEM((tm + 8, WD), F32)],
        compiler_params=_params("arbitrary"),
        name="front_b",
    )(h, w, qg, kvg, wq, wqs, wkv, cos, sin, inj1, inj2, cw)


FLASH_HEADS = 2


def _flash_kernel(q_ref, k_ref, vt_ref, g_ref, bias_ref, o_ref,
                  s_ref, p_ref, pm_ref, a_ref, m_ref, acc_ref, *, tq, tk, hp):
    qi = pl.program_id(2)
    nk = k_ref.shape[0] // tk
    r = tq // tk
    nt = (((1,), (1,)), ((), ()))

    def qk(j, slot):
        jc = jnp.minimum(j, nk - 1)
        bias = jnp.concatenate([bias_ref[jnp.clip(j - (r * qi + u) + 1, 0, 2)] for u in range(r)], axis=1)
        for h in range(hp):
            kb = k_ref[pl.ds(pl.multiple_of(jc * tk, tk), tk), h * HEAD_PAD:(h + 1) * HEAD_PAD]
            q = q_ref[:, h * HEAD_PAD:(h + 1) * HEAD_PAD]
            s = lax.dot_general(kb, q, nt, preferred_element_type=F32) + bias
            s_ref[slot, h] = s
            pm = s[0:8, :]
            for c in range(1, tk // 8):
                pm = jnp.maximum(pm, s[c * 8:(c + 1) * 8, :])
            pm_ref[slot, h] = pm

    def softmax(slot):
        for h in range(hp):
            m_old = m_ref[h]
            m_new = jnp.maximum(m_old, jnp.max(pm_ref[slot, h], axis=0, keepdims=True))
            a_ref[slot, h] = jnp.exp2(m_old - m_new)
            m_ref[h] = m_new
            p_ref[slot, h] = jnp.exp2(s_ref[slot, h] - m_new).astype(BF16)

    def pv(j, slot):
        jc = jnp.clip(j, 0, nk - 1)
        for h in range(hp):
            acc_ref[h] = a_ref[slot, h] * acc_ref[h] + _dot(vt_ref[jc, h], p_ref[slot, h])

    m_ref[...] = jnp.full_like(m_ref, NEG_INF)
    acc_ref[...] = jnp.zeros_like(acc_ref)
    p_ref[1] = jnp.zeros_like(p_ref[1])
    a_ref[1] = jnp.ones_like(a_ref[1])
    qk(0, 0)

    def pair(jj, c):
        j = 2 * jj
        qk(j + 1, 1)
        softmax(0)
        pv(j - 1, 1)
        qk(j + 2, 0)
        softmax(1)
        pv(j, 0)
        return c

    n_pairs = (r * (qi + 1) + 1) // 2
    lax.fori_loop(0, n_pairs, pair, 0)
    pv(2 * n_pairs - 1, 1)
    for h in range(hp):
        acc = acc_ref[h]
        o = (acc[:VDIM, :] / acc[VDIM:VDIM + 1, :]).T
        cols = slice(h * VDIM, (h + 1) * VDIM)
        o_ref[:, cols] = (o * g_ref[:, cols].astype(F32)).astype(o_ref.dtype)


def _flash(q, k, vt, g, bias, *, batch, seq_len, heads, tq, name):
    tk = vt.shape[3]
    nq, nk = seq_len // tq, seq_len // tk
    hp = FLASH_HEADS
    q_spec = pl.BlockSpec((tq, hp * HEAD_PAD), lambda b, h, i: (b * nq + i, h))
    o_spec = pl.BlockSpec((tq, hp * VDIM), lambda b, h, i: (b * nq + i, h))
    once = pl.Buffered(1)
    return pl.pallas_call(
        functools.partial(_flash_kernel, tq=tq, tk=tk, hp=hp),
        grid=(batch, heads // hp, nq),
        in_specs=[q_spec,
                  pl.BlockSpec((seq_len, hp * HEAD_PAD), lambda b, h, i: (b, h), pipeline_mode=once),
                  pl.BlockSpec((nk, hp, DV_PAD, tk), lambda b, h, i: (b, h, 0, 0), pipeline_mode=once),
                  o_spec,
                  _const_spec(bias.shape)],
        out_specs=o_spec,
        out_shape=jax.ShapeDtypeStruct(g.shape, BF16),
        scratch_shapes=[pltpu.VMEM((2, hp, tk, tq), F32), pltpu.VMEM((2, hp, tk, tq), BF16),
                        pltpu.VMEM((2, hp, 8, tq), F32), pltpu.VMEM((2, hp, 1, tq), F32),
                        pltpu.VMEM((hp, 1, tq), F32), pltpu.VMEM((hp, DV_PAD, tq), F32)],
        compiler_params=_params("parallel", "parallel", "arbitrary"),
        name=name,
    )(q, k, vt, g, bias)


def _mask_bias(t, gran):
    key = np.arange(t)[:, None] // gran
    qry = np.arange(t)[None, :] // gran
    diag = np.where(key <= qry, 0.0, NEG_INF)
    return jnp.asarray(np.stack([np.zeros((t, t)), diag, np.full((t, t), NEG_INF)]), F32)


def _sample_attn_kernel(qa_ref, ka_ref, v_ref, ga_ref, cumc_ref, cumr_ref, ck_ref, cv_ref, clogf_ref, utri_ref,
                        qm_ref, km_ref, vc_ref, gc_ref, cckv_ref, ckr_ref, wkv_ref,
                        oa_ref, oc_ref, *, past_len):
    ts = qa_ref.shape[0]
    row = lax.broadcasted_iota(jnp.int32, (ts, ts), 0)
    col = lax.broadcasted_iota(jnp.int32, (ts, ts), 1)
    nt = (((1,), (1,)), ((), ()))

    def attend(s_c, s_n, v_c, v_n):
        m = jnp.maximum(jnp.max(s_c, axis=-1, keepdims=True), jnp.max(s_n, axis=-1, keepdims=True))
        p_c = jnp.exp2(s_c - m)
        p_n = jnp.exp2(s_n - m)
        l = jnp.sum(p_c, axis=-1, keepdims=True) + jnp.sum(p_n, axis=-1, keepdims=True)
        return (_dot(p_c.astype(BF16), v_c) + _dot(p_n.astype(BF16), v_n)) / l

    utri = utri_ref[...]
    hi, mid, lo = _split3(clogf_ref[...])
    c_cache = ((_dot(hi, utri) + _dot(mid, utri)) + _dot(lo, utri)) * LOG2E
    cum_col = cumc_ref[...]
    cum_row = cumr_ref[...]
    causal = col <= row
    for hh in range(HA):
        q = qa_ref[:, hh * HEAD_PAD:hh * HEAD_PAD + DHA]
        kn = ka_ref[:, hh * HEAD_PAD:hh * HEAD_PAD + DHA]
        kc = ck_ref[pl.ds(hh, past_len, stride=HA), :].astype(BF16)
        vcache = cv_ref[pl.ds(hh, past_len, stride=HA), :].astype(BF16)
        base = c_cache[hh:hh + 1, past_len - 1:past_len]
        cq = cum_col[:, hh:hh + 1] + base
        s_c = lax.dot_general(q, kc, nt, preferred_element_type=F32) + (cq - c_cache[hh:hh + 1, :])
        s_n = lax.dot_general(q, kn, nt, preferred_element_type=F32) + (cq - (cum_row[hh:hh + 1, :] + base))
        s_n = jnp.where(causal, s_n, NEG_INF)
        o = attend(s_c, s_n, vcache, v_ref[:, hh * DHA:(hh + 1) * DHA])
        oa_ref[:, hh * DHA:(hh + 1) * DHA] = (o * ga_ref[:, hh * DHA:(hh + 1) * DHA].astype(F32)).astype(BF16)

    kv_c = _dot(cckv_ref[...].astype(BF16), wkv_ref[...])
    ckr = ckr_ref[...]
    chunk_ok = (past_len + col) // CHUNK <= (past_len + row) // CHUNK
    for hh in range(HC):
        o_ = hh * HEAD_PAD
        qn = qm_ref[:, o_:o_ + NOPE]
        qr = qm_ref[:, o_ + NOPE:o_ + HEAD_PAD]
        s_c = (lax.dot_general(qn, kv_c[:, o_:o_ + NOPE].astype(BF16), nt, preferred_element_type=F32)
               + lax.dot_general(qr, ckr, nt, preferred_element_type=F32))
        s_n = lax.dot_general(qm_ref[:, o_:o_ + HEAD_PAD], km_ref[:, o_:o_ + HEAD_PAD], nt,
                              preferred_element_type=F32)
        s_n = jnp.where(chunk_ok, s_n, NEG_INF)
        o = attend(s_c, s_n, kv_c[:, o_ + NOPE:o_ + HEAD_PAD].astype(BF16), vc_ref[:, hh * VDIM:(hh + 1) * VDIM])
        oc_ref[:, hh * VDIM:(hh + 1) * VDIM] = (o * gc_ref[:, hh * VDIM:(hh + 1) * VDIM].astype(F32)).astype(BF16)


def _sample_attn(qa, ka, v, ga, cum_col, cum_row, ck, cv, clogf, utri, qm, km, vc, gc, cckv, ckr, wkv,
                 *, batch, ts, past_len, layer):
    row = lambda width: pl.BlockSpec((ts, width), lambda b: (b, 0))
    per_b = lambda shape: pl.BlockSpec((None,) + shape, lambda b: (b, 0, 0))
    cache = lambda width: pl.BlockSpec((None, past_len, width), lambda b: (layer * batch + b, 0, 0))
    fox_cache = pl.BlockSpec((None, past_len * HA, DHA), lambda b: (layer * batch + b, 0, 0))
    return pl.pallas_call(
        functools.partial(_sample_attn_kernel, past_len=past_len),
        grid=(batch,),
        in_specs=[row(4 * HEAD_PAD), row(4 * HEAD_PAD), row(WA), row(WA), row(LANE), per_b((8, ts)),
                  fox_cache, fox_cache, per_b((8, past_len)), _const_spec(utri.shape),
                  row(4 * HEAD_PAD), row(4 * HEAD_PAD), row(WC), row(WC),
                  cache(KV_RANK), per_b((past_len, LANE)), _const_spec(wkv.shape)],
        out_specs=[row(WA), row(WC)],
        out_shape=[jax.ShapeDtypeStruct((batch * ts, WA), BF16), jax.ShapeDtypeStruct((batch * ts, WC), BF16)],
        compiler_params=_params("parallel"),
        name="sample_attn",
    )(qa, ka, v, ga, cum_col, cum_row, ck, cv, clogf, utri, qm, km, vc, gc, cckv, ckr, wkv)


def _merge_kernel(h_ref, oa_ref, ob_ref, oc_ref, od_ref, wga_ref, wgb_ref, wgc_ref, wgd_ref, wb_ref, mg_ref):
    h = h_ref[...]
    merged = None
    for br, (o_ref, wg_ref) in enumerate(zip((oa_ref, ob_ref, oc_ref, od_ref),
                                             (wga_ref, wgb_ref, wgc_ref, wgd_ref))):
        term = jax.nn.sigmoid(_dot(h, wg_ref[...])) * _dot(o_ref[...], wb_ref[br])
        merged = term if merged is None else merged + term
    mg_ref[...] = merged.astype(mg_ref.dtype)


def _merge(h, oa, ob, oc, od, wg, wb, *, tm, tj, layer):
    m, d = h.shape
    nj = d // tj
    row = lambda width: pl.BlockSpec((tm, width), lambda j, i: (i, 0))
    gate = lambda br: pl.BlockSpec((None, d, tj), lambda j, i: (layer, 0, br * nj + j))
    return pl.pallas_call(
        _merge_kernel,
        grid=(nj, m // tm),
        in_specs=[row(d), row(WA), row(WB), row(WC), row(WD), gate(0), gate(1), gate(2), gate(3),
                  pl.BlockSpec((N_BRANCH, WA, tj), lambda j, i: (0, 0, j))],
        out_specs=pl.BlockSpec((tm, tj), lambda j, i: (i, j)),
        out_shape=jax.ShapeDtypeStruct((m, d), BF16),
        compiler_params=_params("parallel", "parallel"),
        name="merge",
    )(h, oa, ob, oc, od, wg, wg, wg, wg, wb)


def _out_proj_kernel(mg_ref, x_ref, wo_ref, gn_ref, *out_refs):
    x = x_ref[...] + _dot(mg_ref[...], wo_ref[...])
    y = x * lax.rsqrt(jnp.mean(x * x, axis=-1, keepdims=True) + EPS)
    hn_ref = out_refs[-1]
    hn_ref[...] = (y * gn_ref[...]).astype(hn_ref.dtype)
    if len(out_refs) == 2:
        out_refs[0][...] = x


def _out_proj(mg, x, wo, gn, *, tm, norm_dtype, want_x):
    m, d = x.shape
    row = pl.BlockSpec((tm, d), lambda i: (i, 0))
    shapes = [jax.ShapeDtypeStruct((m, d), norm_dtype)]
    if want_x:
        shapes = [jax.ShapeDtypeStruct((m, d), F32)] + shapes
    outs = pl.pallas_call(
        _out_proj_kernel,
        grid=(m // tm,),
        in_specs=[row, row, _const_spec(wo.shape), _const_spec((1, d))],
        out_specs=[row] * len(shapes),
        out_shape=shapes,
        compiler_params=_params("parallel"),
        name="out_proj",
    )(mg, x, wo, gn)
    return (outs[0], outs[1]) if want_x else (None, outs[0])


def _rope_tables(pos):
    half = ROPE_DIM // 2
    inv = ROPE_BASE ** (-jnp.arange(half, dtype=F32) / half)
    ang = pos.astype(F32)[:, None] * inv[None, :]
    z = jnp.zeros((pos.shape[0], LANE - ROPE_DIM), F32)
    cos = jnp.concatenate([jnp.cos(ang), jnp.cos(ang), z], axis=-1)
    sin = jnp.concatenate([-jnp.sin(ang), jnp.sin(ang), z], axis=-1)
    return cos, sin


def _pad_cols(a, width):
    return jnp.pad(a, ((0, 0), (0, width - a.shape[1])))


def _repack_kernel(tab_ref, w_ref, o_ref):
    keep = tab_ref[1, pl.program_id(0)]
    rows = lax.broadcasted_iota(jnp.int32, (LANE, w_ref.shape[2]), 0)
    for l in range(w_ref.shape[1]):
        y = jnp.where(rows < keep, w_ref[:, l, :], 0.0)
        o_ref[l] = y.T.astype(BF16)


def _repack(wt, table):
    n_in, depth, d = wt.shape
    assert all(0 <= s and s + LANE <= n_in for s, _ in table)
    tab = jnp.asarray(np.asarray(table, np.int32).T)
    return pl.pallas_call(
        _repack_kernel,
        grid_spec=pltpu.PrefetchScalarGridSpec(
            num_scalar_prefetch=1, grid=(len(table),),
            in_specs=[pl.BlockSpec((pl.Element(LANE), pl.Element(depth), pl.Element(d)),
                                   lambda i, t: (t[0, i], 0, 0))],
            out_specs=pl.BlockSpec((depth, d, LANE), lambda i, t: (0, 0, i))),
        out_shape=jax.ShapeDtypeStruct((depth, d, LANE * len(table)), BF16),
        compiler_params=_params("arbitrary"),
        name="repack",
    )(tab, wt)


def _repack_tables(d):
    sizes = (WA, WA, WA, HA, WA, WB, WB, WB, Q_RANK, KV_RANK, ROPE_DIM, WC, WD, WD, WD, WD, N_BRANCH * d)
    (a_q, _, _, a_f, a_g, _, _, _, c_q, _, c_kr, c_g, d_b, _, _, _, m_g, end) = (
        int(o) for o in np.concatenate([[0], np.cumsum(sizes)]))
    run = lambda start, stop: [(s, LANE) for s in range(start, stop, LANE)]
    ta = run(a_q, a_f) + run(a_g, c_q) + [(a_f, HA)]
    tb = run(c_q, c_kr) + run(c_g, d_b) + [(c_kr, ROPE_DIM)] + run(d_b, m_g)
    tg = run(m_g, end)
    assert len(ta) * LANE == A_COLS and len(tb) * LANE == B_COLS
    return ta, tb, tg


def _layer_weights(l, d, fox_fb, gm_ln_g, gm_ln_b, mla_q_norm_g, mla_wq_b, mla_kv_norm_g, mla_wkv_b,
                   conv_w, w_branch, w_out):
    half = ROPE_DIM // 2
    swap = lambda a: jnp.concatenate([a[..., half:], a[..., :half]], axis=-1)
    wq = mla_wq_b[l].reshape(Q_RANK, HC, NOPE + ROPE_DIM)
    zpad = jnp.zeros((Q_RANK, HC, LANE - ROPE_DIM), F32)
    wq2 = jnp.concatenate([wq, zpad], axis=-1).reshape(Q_RANK, HC * HEAD_PAD).astype(BF16)
    wqs = jnp.concatenate([swap(wq[..., NOPE:]), zpad], axis=-1).reshape(Q_RANK, HC * LANE).astype(BF16)
    return dict(
        wq=wq2, wqs=wqs, wkv=mla_wkv_b[l].astype(BF16),
        fb=_pad_cols(fox_fb[l].reshape(1, HA), LANE),
        lng=gm_ln_g[l].reshape(1, WB), lnb=gm_ln_b[l].reshape(1, WB),
        qg=mla_q_norm_g[l].reshape(1, Q_RANK), kvg=mla_kv_norm_g[l].reshape(1, KV_RANK),
        cw=jnp.pad(conv_w[l], ((0, 8 - CONV_W), (0, 0))),
        wbr=w_branch[l].astype(BF16), wo=w_out[l].astype(BF16))


def _group_layer(lw, wts, layer, x, h, gn, norm_dtype, want_x, *, batch, seq_len, tm, cos, sin, inj1, inj2, wc, bc, tri, attn):
    tiles_per_batch = max(seq_len // tm, 1)
    wa_all, wb_all, wg_all = wts
    fa = _front_a(h, wa_all, lw["fb"], lw["lng"], lw["lnb"], wc, bc, tri, tm=tm, tiles_per_batch=tiles_per_batch,
                  layer=layer)
    fb = _front_b(h, wb_all, lw["qg"], lw["kvg"], lw["wq"], lw["wqs"], lw["wkv"], cos, sin, inj1, inj2,
                  lw["cw"], tm=tm, seq_len=seq_len, layer=layer)
    qa, ka, v, ga, k32, v32, logf, cum, ob, vn, vta = fa
    qm, km, vc, gc, ckv32, kr32, od, cin32, vtc = fb
    oa, oc = attn(fa, fb)
    m, d = x.shape
    tm_merge = 2 * tm if m % (2 * tm) == 0 else tm
    mg = _merge(h, oa, ob, oc, od, wg_all, lw["wbr"], tm=tm_merge, tj=min(512, d), layer=layer)
    x_new, normed = _out_proj(mg, x, lw["wo"], gn.reshape(1, d), tm=tm, norm_dtype=norm_dtype, want_x=want_x)
    r3 = lambda a: a.reshape(batch, seq_len, a.shape[-1])
    states = dict(k=r3(k32), v=r3(v32), logf=r3(logf)[..., :HA], ckv=r3(ckv32), kr=r3(kr32)[..., :ROPE_DIM],
                  cin=r3(cin32), vn=r3(vn))
    return x_new, normed, states


def kernel(x_prompt, x_sample, cache_fox_k, cache_fox_v, cache_fox_logf, cache_mla_ckv, cache_mla_krope, state_conv, norm_g, w_in, fox_fb, gm_ln_g, gm_ln_b, gm_ws, gm_bs, mla_q_norm_g, mla_wq_b, mla_kv_norm_g, mla_wkv_b, conv_w, w_branch, w_out, final_norm_g):
    bp, tp, d = x_prompt.shape
    bs, ts, _ = x_sample.shape
    depth = w_in.shape[0]
    past = cache_fox_k.shape[2]
    keep = min(tp, past)
    tm_p = min(512, tp)
    tm_s = bs * ts
    t_attn = tm_p
    tq_attn = 2 * t_attn if tp % (2 * t_attn) == 0 else t_attn
    assert tp % tm_p == 0 and tm_p % GM_CHUNK == 0 and t_attn % CHUNK == 0
    assert ts <= GM_CHUNK and tm_s % 8 == 0 and past % CHUNK == 0

    xp = x_prompt.reshape(bp * tp, d)
    xs = x_sample.reshape(bs * ts, d)

    cos_p, sin_p = _rope_tables(jnp.arange(tp, dtype=jnp.int32))
    cos_s, sin_s = _rope_tables(past + jnp.arange(ts, dtype=jnp.int32))
    cos_s, sin_s = jnp.tile(cos_s, (bs, 1)), jnp.tile(sin_s, (bs, 1))
    ar_p = np.arange(tm_p)
    tri_p = jnp.asarray(ar_p[None, :] <= ar_p[:, None], BF16)
    ar_s = np.arange(tm_s)
    same_b = (ar_s[None, :] // ts) == (ar_s[:, None] // ts)
    tri_s = jnp.asarray(same_b & (ar_s[None, :] <= ar_s[:, None]), BF16)
    ar_c = np.arange(past)
    utri_c = jnp.asarray(ar_c[:, None] <= ar_c[None, :], BF16)
    zero_inj = jnp.zeros((tm_p, WD), F32)
    tril = np.tril(np.ones((GM_CHUNK, GM_CHUNK), bool))
    bias_fox = _mask_bias(t_attn, 1)
    bias_mla = _mask_bias(t_attn, CHUNK)

    wt = jnp.transpose(w_in, (2, 0, 1))
    wts = tuple(_repack(wt, tab) for tab in _repack_tables(d))

    hp = _rmsnorm(xp, norm_g[0], tm_p, BF16)
    hs = _rmsnorm(xs, norm_g[0], tm_s, BF16)
    st_p, st_s = [], []
    for l in range(depth):
        lw = _layer_weights(l, d, fox_fb, gm_ln_g, gm_ln_b, mla_q_norm_g, mla_wq_b, mla_kv_norm_g,
                            mla_wkv_b, conv_w, w_branch, w_out)
        last = l == depth - 1
        gn = final_norm_g if last else norm_g[l + 1]
        norm_dtype = F32 if last else BF16

        wm = jnp.where(tril, gm_ws[l], 0.0)
        wc_p = wm.astype(BF16)
        bc_p = jnp.repeat(gm_bs[l].T, GC, axis=1)
        wc_s = jnp.einsum("ab,gts->gatbs", jnp.eye(bs, dtype=F32), wm[:, :ts, :ts]).reshape(
            GB, tm_s, tm_s).astype(BF16)
        bc_s = jnp.tile(bc_p[:ts], (bs, 1))

        def attn_p(fa, fb):
            oa = _flash(fa[0], fa[1], fa[10], fa[3], bias_fox, batch=bp, seq_len=tp, heads=HA, tq=tq_attn,
                        name="flash_fox")
            oc = _flash(fb[0], fb[1], fb[8], fb[3], bias_mla, batch=bp, seq_len=tp, heads=HC, tq=tq_attn,
                        name="flash_mla")
            return oa, oc

        xp, hp, sp = _group_layer(lw, wts, l, xp, hp, gn, norm_dtype, not last, batch=bp, seq_len=tp, tm=tm_p,
                                  cos=cos_p, sin=sin_p, inj1=zero_inj, inj2=zero_inj, wc=wc_p, bc=bc_p,
                                  tri=tri_p, attn=attn_p)

        sc = state_conv[l]
        inj1 = jnp.zeros((bs, ts, WD), F32).at[:, 0].set(sc[:, 1]).reshape(tm_s, WD)
        inj2 = jnp.zeros((bs, ts, WD), F32).at[:, 0].set(sc[:, 0]).at[:, 1].set(sc[:, 1]).reshape(tm_s, WD)
        clogf = jnp.pad(jnp.transpose(cache_fox_logf[l], (0, 2, 1)), ((0, 0), (0, 8 - HA), (0, 0)))
        ckr = jnp.pad(cache_mla_krope[l], ((0, 0), (0, 0), (0, LANE - ROPE_DIM))).astype(BF16)

        def attn_s(fa, fb):
            cum_row = jnp.transpose(fa[7].reshape(bs, ts, LANE)[:, :, :8], (0, 2, 1))
            return _sample_attn(fa[0], fa[1], fa[2], fa[3], fa[7], cum_row,
                                cache_fox_k.reshape(depth * bs, past * HA, DHA),
                                cache_fox_v.reshape(depth * bs, past * HA, DHA),
                                clogf, utri_c, fb[0], fb[1], fb[2], fb[3],
                                cache_mla_ckv.reshape(depth * bs, past, KV_RANK), ckr, lw["wkv"],
                                batch=bs, ts=ts, past_len=past, layer=l)

        xs, hs, ss = _group_layer(lw, wts, l, xs, hs, gn, norm_dtype, not last, batch=bs, seq_len=ts, tm=tm_s,
                                  cos=cos_s, sin=sin_s, inj1=inj1, inj2=inj2, wc=wc_s, bc=bc_s, tri=tri_s,
                                  attn=attn_s)
        st_p.append(sp)
        st_s.append(ss)

    y_prompt = hp.reshape(bp, tp, d)
    y_sample = hs.reshape(bs, ts, d)
    stk = lambda sts, f: jnp.stack([f(s) for s in sts])
    p_out = (stk(st_p, lambda s: s["k"][:, tp - keep:].reshape(bp, keep, HA, DHA)),
             stk(st_p, lambda s: s["v"][:, tp - keep:].reshape(bp, keep, HA, DHA)),
             stk(st_p, lambda s: s["logf"][:, tp - keep:]),
             stk(st_p, lambda s: s["ckv"][:, tp - keep:]),
             stk(st_p, lambda s: s["kr"][:, tp - keep:]),
             stk(st_p, lambda s: s["cin"][:, tp - (CONV_W - 1):]),
             stk(st_p, lambda s: s["vn"][:, tp - GM_CHUNK:]))
    s_out = (stk(st_s, lambda s: s["k"].reshape(bs, ts, HA, DHA)),
             stk(st_s, lambda s: s["v"].reshape(bs, ts, HA, DHA)),
             stk(st_s, lambda s: s["logf"]),
             stk(st_s, lambda s: s["ckv"]),
             stk(st_s, lambda s: s["kr"]),
             stk(st_s, lambda s: s["cin"][:, ts - (CONV_W - 1):]),
             stk(st_s, lambda s: s["vn"]))
    return (y_prompt, y_sample) + p_out + s_out
```
